```python
import math
import jax
import jax.numpy as jnp
from jax import lax
import numpy as np

D_MODEL = 1024
BATCH = 4
SEQ = 8192
DEPTH = 2

GRID_W = 64
CTX_LEN = 256
EPS = 1e-6
N_MOD = 9
D_FF = 2816
CONV_K = 5
GDN_HEADS = 4
GDN_DK = 128
GDN_DV = 128
GDN_CHUNK = 64
DIFF_HEADS = 4
DIFF_DQK = 64
DIFF_DV = 2 * DIFF_DQK
Q_BLOCK = 128
ROPE_BASE = 10000.0
ROPE_PAIRS_PER_AXIS = DIFF_DQK // 4
SSD_HEADS = 8
SSD_HEADDIM = 64
SSD_INNER = SSD_HEADS * SSD_HEADDIM
SSD_GROUPS = 2
SSD_STATE = 128
SSD_CHUNK = 128
N_BRANCH = 3
BRANCH_W = 512
GDN_QKV = 2 * GDN_HEADS * GDN_DK + GDN_HEADS * GDN_DV
GDN_VW = GDN_HEADS * GDN_DV
DIFF_QK = DIFF_HEADS * 2 * DIFF_DQK
DIFF_VW = DIFF_HEADS * DIFF_DV
SSD_XBC = SSD_INNER + 2 * SSD_GROUPS * SSD_STATE
SPLIT_SIZES = (GDN_QKV, GDN_VW, 2 * GDN_HEADS, 2 * GDN_HEADS, DIFF_QK, DIFF_QK, DIFF_VW, SSD_INNER, SSD_XBC, 2 * SSD_HEADS, N_BRANCH * D_MODEL)
PROJ_W = sum(SPLIT_SIZES)

kernel_name = 'hybrid_gdn_diffattn_ssd_dit_block'


def split_cols(t, sizes):
    idx = [int(v) for v in np.cumsum(sizes)[:-1]]
    return jnp.split(t, idx, axis=-1)


def rms_norm(x, g):
    xf = x.astype(jnp.float32)
    y = xf * lax.rsqrt(jnp.mean(xf * xf, axis=-1, keepdims=True) + EPS)
    return (y * g.astype(jnp.float32)).astype(x.dtype)


def l2norm(x):
    return x * lax.rsqrt(jnp.sum(x * x, axis=-1, keepdims=True) + EPS)


def modulate(h, shift, scale):
    return h * (1 + scale) + shift


def swiglu(h, w1, w2):
    g, u = jnp.split(h @ w1, 2, axis=-1)
    return (jax.nn.silu(g) * u) @ w2


def flip_seq(t, rev):
    return jnp.flip(t, axis=1) if rev else t


def dw_conv_centred(x, w, b=None):
    pad = CONV_K // 2
    y = lax.conv_general_dilated(x, w.astype(x.dtype)[:, None, :], window_strides=(1,), padding=[(pad, pad)],
                                 dimension_numbers=('NWC', 'WIO', 'NWC'), feature_group_count=x.shape[-1])
    return y if b is None else y + b.astype(x.dtype)


def segsum(a):
    t = a.shape[-1]
    ae = jnp.broadcast_to(a[..., :, None], a.shape + (t,))
    cs = jnp.cumsum(jnp.where(jnp.tril(jnp.ones((t, t), bool), -1), ae, 0.0), axis=-2)
    return jnp.where(jnp.tril(jnp.ones((t, t), bool)), cs, -jnp.inf)


def gated_delta_chunked(q, k, v, g, beta, s0):
    b, L, H, dk = q.shape
    dv = v.shape[-1]
    n = L // GDN_CHUNK

    def blk(t):
        return jnp.moveaxis(t.reshape((b, n, GDN_CHUNK) + t.shape[2:]), 3, 1)

    q, k, v, g, beta = blk(q), blk(k), blk(v), blk(g), blk(beta)
    gc = jnp.cumsum(g, axis=-1)
    decay = jnp.exp(segsum(g))
    kb = k * beta[..., None]
    vb = v * beta[..., None]
    strict = jnp.tril(jnp.ones((GDN_CHUNK, GDN_CHUNK), bool), -1)
    m = jnp.where(strict, jnp.einsum('bhnid,bhnjd->bhnij', kb, k) * decay, 0.0)
    a_mat = m + jnp.eye(GDN_CHUNK, dtype=m.dtype)
    u = lax.linalg.triangular_solve(a_mat, vb, left_side=True, lower=True, unit_diagonal=True)
    w = lax.linalg.triangular_solve(a_mat, kb * jnp.exp(gc)[..., None], left_side=True, lower=True, unit_diagonal=True)
    attn = jnp.einsum('bhnid,bhnjd->bhnij', q, k) * decay
    q_dec = q * jnp.exp(gc)[..., None]
    k_dec = k * jnp.exp(gc[..., -1:] - gc)[..., None]
    g_tot = jnp.exp(gc[..., -1])

    def step(s, inp):
        u_c, w_c, q_c, a_c, k_c, gt_c = inp
        v_new = u_c - jnp.einsum('bhid,bhdv->bhiv', w_c, s)
        o = jnp.einsum('bhid,bhdv->bhiv', q_c, s) + jnp.einsum('bhij,bhjv->bhiv', a_c, v_new)
        s = s * gt_c[..., None, None] + jnp.einsum('bhid,bhiv->bhdv', k_c, v_new)
        return s, o

    xs = (jnp.moveaxis(u, 2, 0), jnp.moveaxis(w, 2, 0), jnp.moveaxis(q_dec, 2, 0),
          jnp.moveaxis(attn, 2, 0), jnp.moveaxis(k_dec, 2, 0), jnp.moveaxis(g_tot, 2, 0))
    s_fin, o = lax.scan(step, s0, xs)
    return jnp.transpose(o, (1, 0, 3, 2, 4)).reshape(b, L, H, dv), s_fin


def gdn_prep(qkv, a, b_raw, conv_w, a_log, dt_bias):
    bsz, L = qkv.shape[:2]
    qkv = jax.nn.silu(dw_conv_centred(qkv, conv_w)).astype(jnp.float32)
    q, k, v = split_cols(qkv, (GDN_HEADS * GDN_DK, GDN_HEADS * GDN_DK, GDN_HEADS * GDN_DV))
    q = l2norm(q.reshape(bsz, L, GDN_HEADS, GDN_DK)) * GDN_DK ** -0.5
    k = l2norm(k.reshape(bsz, L, GDN_HEADS, GDN_DK))
    v = v.reshape(bsz, L, GDN_HEADS, GDN_DV)
    g = -jnp.exp(a_log.astype(jnp.float32)) * jax.nn.softplus(a.astype(jnp.float32).reshape(bsz, L, 2, GDN_HEADS) + dt_bias.astype(jnp.float32))
    beta = jax.nn.sigmoid(b_raw.astype(jnp.float32).reshape(bsz, L, 2, GDN_HEADS))
    return q, k, v, g, beta


def gdn_out(o, z, norm_w):
    bsz, L = o.shape[:2]
    o = rms_norm(o, norm_w) * jax.nn.silu(z.astype(jnp.float32)).reshape(bsz, L, GDN_HEADS, GDN_DV)
    return o.reshape(bsz, L, GDN_VW).astype(z.dtype)


def gdn_mixer(qkv_x, z_x, a_x, b_x, qkv_c, z_c, a_c, b_c, conv_w, a_log, dt_bias, norm_w, with_ctx):
    qx, kx, vx, gx, bx = gdn_prep(qkv_x, a_x, b_x, conv_w, a_log, dt_bias)
    qc, kc, vc, gcx, bcx = gdn_prep(qkv_c, a_c, b_c, conv_w, a_log, dt_bias)
    s0 = jnp.zeros((qx.shape[0], GDN_HEADS, GDN_DK, GDN_DV), jnp.float32)
    ox, oc = 0.0, 0.0
    for d in range(2):
        rev = d == 1
        o_cd, s_d = gated_delta_chunked(flip_seq(qc, rev), flip_seq(kc, rev), flip_seq(vc, rev),
                                        flip_seq(gcx[:, :, d], rev), flip_seq(bcx[:, :, d], rev), s0)
        o_xd, _ = gated_delta_chunked(flip_seq(qx, rev), flip_seq(kx, rev), flip_seq(vx, rev),
                                      flip_seq(gx[:, :, d], rev), flip_seq(bx[:, :, d], rev), s_d)
        ox = ox + flip_seq(o_xd, rev)
        oc = oc + flip_seq(o_cd, rev)
    out_c = gdn_out(oc, z_c, norm_w) if with_ctx else None
    return gdn_out(ox, z_x, norm_w), out_c


def axial_rope_angles(rows):
    row = jnp.repeat(jnp.arange(rows, dtype=jnp.float32), GRID_W)
    col = jnp.tile(jnp.arange(GRID_W, dtype=jnp.float32), rows)
    inv = ROPE_BASE ** (-jnp.arange(ROPE_PAIRS_PER_AXIS, dtype=jnp.float32) / ROPE_PAIRS_PER_AXIS)
    ang = jnp.concatenate([row[:, None] * inv, col[:, None] * inv], axis=-1)
    return jnp.cos(ang), jnp.sin(ang)


def apply_rope(t, cos, sin):
    tp = t.reshape(t.shape[:-1] + (DIFF_DQK // 2, 2))
    t0, t1 = tp[..., 0], tp[..., 1]
    c = cos[:, None, None, :].astype(t.dtype)
    s = sin[:, None, None, :].astype(t.dtype)
    return jnp.stack([t0 * c - t1 * s, t0 * s + t1 * c], axis=-1).reshape(t.shape)


def diff_attend(q, k, v, lam_full):
    s = jnp.einsum('bqhjd,bkhjd->bhjqk', q, k).astype(jnp.float32) * DIFF_DQK ** -0.5
    p = jax.nn.softmax(s, axis=-1)
    pd = (p[:, :, 0] - lam_full * p[:, :, 1]).astype(v.dtype)
    return jnp.einsum('bhqk,bkhv->bqhv', pd, v)


def diff_out(o, norm_w, lam_init):
    return (rms_norm(o, norm_w) * (1.0 - lam_init)).reshape(o.shape[0], o.shape[1], DIFF_VW)


def diff_mixer(q_x, k_x, v_x, q_c, k_c, v_c, lam, norm_w, lam_init, cos, sin, with_ctx):
    bsz, L = q_x.shape[:2]

    def heads(q, k, v):
        n = q.shape[1]
        return (q.reshape(bsz, n, DIFF_HEADS, 2, DIFF_DQK), k.reshape(bsz, n, DIFF_HEADS, 2, DIFF_DQK),
                v.reshape(bsz, n, DIFF_HEADS, DIFF_DV))

    qx, kx, vx = heads(q_x, k_x, v_x)
    qc, kc, vc = heads(q_c, k_c, v_c)
    qx = apply_rope(qx, cos, sin)
    kx = apply_rope(kx, cos, sin)
    lam = lam.astype(jnp.float32)
    lam_full = jnp.exp(jnp.sum(lam[0] * lam[1])) - jnp.exp(jnp.sum(lam[2] * lam[3])) + lam_init
    k_all = jnp.concatenate([kx, kc], axis=1)
    v_all = jnp.concatenate([vx, vc], axis=1)
    qblocks = jnp.moveaxis(qx.reshape(bsz, L // Q_BLOCK, Q_BLOCK, DIFF_HEADS, 2, DIFF_DQK), 1, 0)
    ox = lax.map(lambda qb: diff_attend(qb, k_all, v_all, lam_full), qblocks)
    ox = jnp.moveaxis(ox, 0, 1).reshape(bsz, L, DIFF_HEADS, DIFF_DV)
    out_c = diff_out(diff_attend(qc, kc, vc, lam_full), norm_w, lam_init) if with_ctx else None
    return diff_out(ox, norm_w, lam_init), out_c


def ssd_chunked(xdt, a, bm, cm, s0):
    b, L, H, P = xdt.shape
    N = bm.shape[-1]
    n = L // SSD_CHUNK
    x = xdt.reshape(b, n, SSD_CHUNK, H, P)
    bc = bm.reshape(b, n, SSD_CHUNK, H, N)
    cc = cm.reshape(b, n, SSD_CHUNK, H, N)
    a = jnp.moveaxis(a.reshape(b, n, SSD_CHUNK, H), 3, 1)
    a_cs = jnp.cumsum(a, axis=-1)
    scores = jnp.einsum('bclhn,bcshn->bhcls', cc, bc) * jnp.exp(segsum(a))
    y_diag = jnp.einsum('bhcls,bcshp->bclhp', scores, x)
    decay_states = jnp.moveaxis(jnp.exp(a_cs[..., -1:] - a_cs), 1, 3)[..., None]
    states = jnp.einsum('bclhn,bclhp->bchpn', bc * decay_states, x)
    states = jnp.concatenate([s0[:, None], states], axis=1)
    chunk_a = jnp.pad(a_cs[..., -1], ((0, 0), (0, 0), (1, 0)))
    new_states = jnp.einsum('bhzc,bchpn->bzhpn', jnp.exp(segsum(chunk_a)), states)
    prev_states, s_fin = new_states[:, :-1], new_states[:, -1]
    y_off = jnp.einsum('bclhn,bchpn->bclhp', cc * jnp.moveaxis(jnp.exp(a_cs), 1, 3)[..., None], prev_states)
    return (y_diag + y_off).reshape(b, L, H, P), s_fin


def ssd_prep(xbc, dt_raw, conv_w, conv_b, dt_bias):
    bsz, L = xbc.shape[:2]
    xbc = jax.nn.silu(dw_conv_centred(xbc, conv_w, conv_b)).astype(jnp.float32)
    xs, bm, cm = split_cols(xbc, (SSD_INNER, SSD_GROUPS * SSD_STATE, SSD_GROUPS * SSD_STATE))
    hpg = SSD_HEADS // SSD_GROUPS
    xs = xs.reshape(bsz, L, SSD_HEADS, SSD_HEADDIM)
    bm = jnp.repeat(bm.reshape(bsz, L, SSD_GROUPS, SSD_STATE), hpg, axis=2)
    cm = jnp.repeat(cm.reshape(bsz, L, SSD_GROUPS, SSD_STATE), hpg, axis=2)
    dt = jax.nn.softplus(dt_raw.astype(jnp.float32).reshape(bsz, L, 2, SSD_HEADS) + dt_bias.astype(jnp.float32))
    return xs, bm, cm, dt


def ssd_out(y, z, norm_w):
    bsz, L = y.shape[:2]
    y = y.reshape(bsz, L, SSD_INNER) * jax.nn.silu(z.astype(jnp.float32))
    return rms_norm(y, norm_w).astype(z.dtype)


def ssd_mixer(z_x, xbc_x, dt_x, z_c, xbc_c, dt_c, conv_w, conv_b, a_log, dt_bias, d_skip, norm_w, with_ctx):
    a = -jnp.exp(a_log.astype(jnp.float32))
    xs_x, bm_x, cm_x, dtx = ssd_prep(xbc_x, dt_x, conv_w, conv_b, dt_bias)
    xs_c, bm_c, cm_c, dtc = ssd_prep(xbc_c, dt_c, conv_w, conv_b, dt_bias)
    s0 = jnp.zeros((xs_x.shape[0], SSD_HEADS, SSD_HEADDIM, SSD_STATE), jnp.float32)
    dsk = d_skip.astype(jnp.float32)[:, None]
    yx, yc = dsk * xs_x, dsk * xs_c
    for d in range(2):
        rev = d == 1
        y_cd, s_d = ssd_chunked(flip_seq(xs_c * dtc[:, :, d, :, None], rev), flip_seq(dtc[:, :, d] * a[d], rev),
                                flip_seq(bm_c, rev), flip_seq(cm_c, rev), s0)
        y_xd, _ = ssd_chunked(flip_seq(xs_x * dtx[:, :, d, :, None], rev), flip_seq(dtx[:, :, d] * a[d], rev),
                              flip_seq(bm_x, rev), flip_seq(cm_x, rev), s_d)
        yx = yx + flip_seq(y_xd, rev)
        yc = yc + flip_seq(y_cd, rev)
    out_c = ssd_out(yc, z_c, norm_w) if with_ctx else None
    return ssd_out(yx, z_x, norm_w), out_c


def merge_branches(outs, gate_raw, w_branch, w_out):
    gates = jax.nn.sigmoid(gate_raw.astype(jnp.float32)).astype(gate_raw.dtype)
    y = gates[..., :D_MODEL] * (outs[0] @ w_branch[0])
    for n in range(1, N_BRANCH):
        y = y + gates[..., n * D_MODEL:(n + 1) * D_MODEL] * (outs[n] @ w_branch[n])
    return y @ w_out


def token_mixing(hx, hc, with_ctx, lam_init, cos, sin, w_in, gdn_conv_w, gdn_a_log, gdn_dt_bias, gdn_norm_w,
                 diff_lambda, diff_norm_w, ssd_conv_w, ssd_conv_b, ssd_a_log, ssd_dt_bias, ssd_d, ssd_norm_w,
                 w_branch, w_out):
    (gqkv_x, gz_x, ga_x, gb_x, dq_x, dk_x, dv_x, sz_x, sxbc_x, sdt_x, gate_x) = split_cols(hx @ w_in, SPLIT_SIZES)
    (gqkv_c, gz_c, ga_c, gb_c, dq_c, dk_c, dv_c, sz_c, sxbc_c, sdt_c, gate_c) = split_cols(hc @ w_in, SPLIT_SIZES)
    oa_x, oa_c = gdn_mixer(gqkv_x, gz_x, ga_x, gb_x, gqkv_c, gz_c, ga_c, gb_c,
                           gdn_conv_w, gdn_a_log, gdn_dt_bias, gdn_norm_w, with_ctx)
    ob_x, ob_c = diff_mixer(dq_x, dk_x, dv_x, dq_c, dk_c, dv_c, diff_lambda, diff_norm_w, lam_init, cos, sin, with_ctx)
    oc_x, oc_c = ssd_mixer(sz_x, sxbc_x, sdt_x, sz_c, sxbc_c, sdt_c, ssd_conv_w, ssd_conv_b,
                           ssd_a_log, ssd_dt_bias, ssd_d, ssd_norm_w, with_ctx)
    yx = merge_branches((oa_x, ob_x, oc_x), gate_x, w_branch, w_out)
    yc = merge_branches((oa_c, ob_c, oc_c), gate_c, w_branch, w_out) if with_ctx else None
    return yx, yc


def half_ffn(h, g_pre, g_post, shift, scale, gate, w1, w2):
    return 0.5 * gate * rms_norm(swiglu(modulate(rms_norm(h, g_pre), shift, scale), w1, w2), g_post)


def setup_inputs(seed: int = 0) -> dict:
    key = jax.random.key(seed)
    ks = jax.random.split(key, 24)
    f32 = jnp.float32

    def nrm(k, shape, scale):
        return jax.random.normal(k, shape, f32) * scale

    def dt_bias_init(k, shape):
        dt = jnp.exp(jax.random.uniform(k, shape, f32, math.log(1e-3), math.log(1e-1)))
        return dt + jnp.log(-jnp.expm1(-dt))

    return {
        'x': nrm(ks[0], (BATCH, SEQ, D_MODEL), 1.0),
        'c': nrm(ks[1], (BATCH, D_MODEL), 1.0),
        'ctx': nrm(ks[2], (BATCH, CTX_LEN, D_MODEL), 1.0),
        'c_ctx': nrm(ks[3], (D_MODEL,), 1.0),
        'w_ada': nrm(ks[4], (DEPTH, D_MODEL, N_MOD * D_MODEL), 0.5 * D_MODEL ** -0.5),
        'b_ada': nrm(ks[5], (DEPTH, N_MOD * D_MODEL), 0.02),
        'norm_g': 1.0 + nrm(ks[6], (DEPTH, 6, D_MODEL), 0.05),
        'w_ffn_in': nrm(ks[7], (DEPTH, 2, D_MODEL, 2 * D_FF), D_MODEL ** -0.5),
        'w_ffn_out': nrm(ks[8], (DEPTH, 2, D_FF, D_MODEL), D_FF ** -0.5),
        'w_in': nrm(ks[9], (DEPTH, D_MODEL, PROJ_W), D_MODEL ** -0.5),
        'gdn_conv_w': nrm(ks[10], (DEPTH, CONV_K, GDN_QKV), CONV_K ** -0.5),
        'gdn_a_log': jnp.log(jax.random.uniform(ks[11], (DEPTH, 2, GDN_HEADS), f32, 1.0, 16.0)),
        'gdn_dt_bias': dt_bias_init(ks[12], (DEPTH, 2, GDN_HEADS)),
        'gdn_norm_w': 1.0 + nrm(ks[13], (DEPTH, GDN_DV), 0.05),
        'diff_lambda': nrm(ks[14], (DEPTH, 4, DIFF_DQK), 0.1),
        'diff_norm_w': 1.0 + nrm(ks[15], (DEPTH, DIFF_DV), 0.05),
        'ssd_conv_w': nrm(ks[16], (DEPTH, CONV_K, SSD_XBC), CONV_K ** -0.5),
        'ssd_conv_b': nrm(ks[17], (DEPTH, SSD_XBC), 0.02),
        'ssd_a_log': jnp.log(jax.random.uniform(ks[18], (DEPTH, 2, SSD_HEADS), f32, 1.0, 16.0)),
        'ssd_dt_bias': dt_bias_init(ks[19], (DEPTH, 2, SSD_HEADS)),
        'ssd_d': 1.0 + nrm(ks[20], (DEPTH, SSD_HEADS), 0.1),
        'ssd_norm_w': 1.0 + nrm(ks[21], (DEPTH, SSD_INNER), 0.05),
        'w_branch': nrm(ks[22], (DEPTH, N_BRANCH, BRANCH_W, D_MODEL), BRANCH_W ** -0.5),
        'w_out': nrm(ks[23], (DEPTH, D_MODEL, D_MODEL), D_MODEL ** -0.5),
    }


def reference(x, c, ctx, c_ctx, w_ada, b_ada, norm_g, w_ffn_in, w_ffn_out, w_in,
              gdn_conv_w, gdn_a_log, gdn_dt_bias, gdn_norm_w, diff_lambda, diff_norm_w,
              ssd_conv_w, ssd_conv_b, ssd_a_log, ssd_dt_bias, ssd_d, ssd_norm_w, w_branch, w_out):
    bsz, n_lat = x.shape[:2]
    rows = n_lat // GRID_W
    cos, sin = axial_rope_angles(rows)
    sc = jax.nn.silu(c)
    scc = jax.nn.silu(c_ctx)
    for i in range(DEPTH):
        last = i == DEPTH - 1
        lam_init = 0.8 - 0.6 * math.exp(-0.3 * i)
        mx = (sc @ w_ada[i] + b_ada[i]).reshape(bsz, N_MOD, 1, D_MODEL)
        mc = (scc @ w_ada[i] + b_ada[i]).reshape(N_MOD, D_MODEL)
        ng = norm_g[i]
        x = x + half_ffn(x, ng[0], ng[1], mx[:, 0], mx[:, 1], mx[:, 2], w_ffn_in[i, 0], w_ffn_out[i, 0])
        ctx = ctx + half_ffn(ctx, ng[0], ng[1], mc[0], mc[1], mc[2], w_ffn_in[i, 0], w_ffn_out[i, 0])
        hx = modulate(rms_norm(x, ng[2]), mx[:, 3], mx[:, 4])
        hc = modulate(rms_norm(ctx, ng[2]), mc[3], mc[4])
        yx, yc = token_mixing(hx, hc, not last, lam_init, cos, sin, w_in[i], gdn_conv_w[i], gdn_a_log[i],
                              gdn_dt_bias[i], gdn_norm_w[i], diff_lambda[i], diff_norm_w[i], ssd_conv_w[i],
                              ssd_conv_b[i], ssd_a_log[i], ssd_dt_bias[i], ssd_d[i], ssd_norm_w[i],
                              w_branch[i], w_out[i])
        x = x + mx[:, 5] * rms_norm(yx, ng[3])
        x = x + half_ffn(x, ng[4], ng[5], mx[:, 6], mx[:, 7], mx[:, 8], w_ffn_in[i, 1], w_ffn_out[i, 1])
        if not last:
            ctx = ctx + mc[5] * rms_norm(yc, ng[3])
            ctx = ctx + half_ffn(ctx, ng[4], ng[5], mc[6], mc[7], mc[8], w_ffn_in[i, 1], w_ffn_out[i, 1])
    return x
```

```python
import functools
import math

import jax
import jax.numpy as jnp
from jax import lax
from jax.experimental import pallas as pl
from jax.experimental.pallas import tpu as pltpu

F32 = jnp.float32
BF16 = jnp.bfloat16

D_MODEL = 1024
DEPTH = 2
GRID_W = 64
EPS = 1e-6
N_MOD = 9
D_FF = 2816
CONV_K = 5
GDN_HEADS = 4
GDN_DK = 128
GDN_DV = 128
GDN_CHUNK = 64
DIFF_HEADS = 4
DIFF_DQK = 64
DIFF_DV = 128
ROPE_BASE = 10000.0
ROPE_PAIRS_PER_AXIS = DIFF_DQK // 4
SSD_HEADS = 8
SSD_HEADDIM = 64
SSD_INNER = SSD_HEADS * SSD_HEADDIM
SSD_GROUPS = 2
SSD_STATE = 128
SSD_CHUNK = 128
N_BRANCH = 3
BRANCH_W = 512

GDN_QKV = 2 * GDN_HEADS * GDN_DK + GDN_HEADS * GDN_DV
SSD_XBC = SSD_INNER + 2 * SSD_GROUPS * SSD_STATE
CONV_W = GDN_QKV + SSD_XBC
SMALL_W = 512
F_SMALL = CONV_W
F_DQ = F_SMALL + SMALL_W
F_W = F_DQ + 2 * 512
H_DV = N_BRANCH * D_MODEL
H_GZ = H_DV + 512
H_SZ = H_GZ + 512
H_W = H_SZ + 512
LANE_GA = 0
LANE_GB = 8
LANE_DT = 16

FF_CHUNK = 256
PROJ_TN = 512
SEQ_T = 256
VMEM_LIMIT = 56 * 1024 * 1024


def _cparams(sem):
    return pltpu.CompilerParams(dimension_semantics=sem, vmem_limit_bytes=VMEM_LIMIT)


def _const_spec(shape):
    nd = len(shape)
    return pl.BlockSpec(shape, lambda *_: (0,) * nd, pipeline_mode=pl.Buffered(1))


def _silu(v):
    return v * jax.nn.sigmoid(v)


def _bdot(a, b):
    return jnp.dot(a.astype(BF16), b.astype(BF16), preferred_element_type=F32)


def _bdot_nt(a, b):
    return lax.dot_general(a.astype(BF16), b.astype(BF16), (((1,), (1,)), ((), ())),
                           preferred_element_type=F32)


def _bdot_tn(a, b):
    return lax.dot_general(a.astype(BF16), b.astype(BF16), (((0,), (0,)), ((), ())),
                           preferred_element_type=F32)


def _rms(v):
    return v * lax.rsqrt(jnp.mean(v * v, axis=-1, keepdims=True) + EPS)


def _mod_kernel(c_ref, w_ref, b_ref, o_ref):
    sc = _silu(c_ref[...])
    o_ref[0, 0] = jnp.dot(sc, w_ref[0], precision=lax.Precision.HIGHEST,
                          preferred_element_type=F32) + b_ref[0]


def _mod_table(cvec, w_ada, b_ada):
    nm = cvec.shape[0]
    out = pl.pallas_call(
        _mod_kernel,
        grid=(DEPTH, N_MOD),
        in_specs=[
            pl.BlockSpec((nm, D_MODEL), lambda l, j: (0, 0)),
            pl.BlockSpec((1, D_MODEL, D_MODEL), lambda l, j: (l, 0, j)),
            pl.BlockSpec((1, 1, D_MODEL), lambda l, j: (l, 0, j)),
        ],
        out_specs=pl.BlockSpec((1, 1, nm, D_MODEL), lambda l, j: (l, j, 0, 0)),
        out_shape=jax.ShapeDtypeStruct((DEPTH, N_MOD, nm, D_MODEL), F32),
        compiler_params=_cparams(("parallel", "parallel")),
        name="mod_table",
    )(cvec, w_ada, b_ada.reshape(DEPTH, 1, N_MOD * D_MODEL))
    return jnp.transpose(out, (0, 2, 1, 3))


def _ffn_kernel(x_ref, mod_ref, ng_ref, w1_ref, w2_ref, o_ref, *, slot, gslot):
    x = x_ref[...]
    m = mod_ref[0]
    shift, scale, gate = m[slot:slot + 1], m[slot + 1:slot + 2], m[slot + 2:slot + 3]
    h = _rms(x) * ng_ref[gslot:gslot + 1, :]
    hb = (h * (1.0 + scale) + shift).astype(BF16)
    acc = None
    for c in range(D_FF // FF_CHUNK):
        lo = c * FF_CHUNK
        g = jnp.dot(hb, w1_ref[:, lo:lo + FF_CHUNK], preferred_element_type=F32)
        u = jnp.dot(hb, w1_ref[:, D_FF + lo:D_FF + lo + FF_CHUNK], preferred_element_type=F32)
        a = (_silu(g) * u).astype(BF16)
        part = jnp.dot(a, w2_ref[lo:lo + FF_CHUNK, :], preferred_element_type=F32)
        acc = part if acc is None else acc + part
    y = _rms(acc) * ng_ref[gslot + 1:gslot + 2, :]
    o_ref[...] = x + 0.5 * gate * y


def _half_ffn(s, mod, ng, w1, w2, *, slot, gslot, tm, n_tiles, tiles_per_sample, n_samples):
    rows = n_tiles * tm
    return pl.pallas_call(
        functools.partial(_ffn_kernel, slot=slot, gslot=gslot),
        grid=(n_tiles,),
        in_specs=[
            pl.BlockSpec((tm, D_MODEL), lambda i: (i, 0)),
            pl.BlockSpec((1, N_MOD, D_MODEL),
                         lambda i: (jnp.minimum(i // tiles_per_sample, n_samples), 0, 0)),
            _const_spec(ng.shape),
            _const_spec(w1.shape),
            _const_spec(w2.shape),
        ],
        out_specs=pl.BlockSpec((tm, D_MODEL), lambda i: (i, 0)),
        out_shape=jax.ShapeDtypeStruct((rows, D_MODEL), F32),
        compiler_params=_cparams(("parallel",)),
        name="half_ffn",
    )(s, mod, ng, w1, w2)


def _inproj_kernel(x_ref, mod_ref, ng_ref, w_ref, of_ref, oh_ref, h_scr, *, nf):
    j = pl.program_id(1)

    @pl.when(j == 0)
    def _():
        m = mod_ref[0]
        h = _rms(x_ref[...]) * ng_ref[2:3, :]
        h_scr[...] = (h * (1.0 + m[4:5]) + m[3:4]).astype(BF16)

    r = jnp.dot(h_scr[...], w_ref[...], preferred_element_type=F32)

    @pl.when(j < nf)
    def _():
        of_ref[...] = r

    @pl.when(j >= nf)
    def _():
        oh_ref[...] = r.astype(BF16)


def _in_proj(s, mod, ng, w, *, tm, n_tiles, tiles_per_sample, n_samples):
    rows = n_tiles * tm
    nf, nh = F_W // PROJ_TN, H_W // PROJ_TN
    return pl.pallas_call(
        functools.partial(_inproj_kernel, nf=nf),
        grid=(n_tiles, nf + nh),
        in_specs=[
            pl.BlockSpec((tm, D_MODEL), lambda i, j: (i, 0)),
            pl.BlockSpec((1, N_MOD, D_MODEL),
                         lambda i, j: (jnp.minimum(i // tiles_per_sample, n_samples), 0, 0)),
            _const_spec(ng.shape),
            pl.BlockSpec((D_MODEL, PROJ_TN), lambda i, j: (0, j)),
        ],
        out_specs=[
            pl.BlockSpec((tm, PROJ_TN), lambda i, j: (i, jnp.minimum(j, nf - 1))),
            pl.BlockSpec((tm, PROJ_TN), lambda i, j: (i, jnp.maximum(j - nf, 0))),
        ],
        out_shape=[jax.ShapeDtypeStruct((rows, F_W), F32),
                   jax.ShapeDtypeStruct((rows, H_W), BF16)],
        scratch_shapes=[pltpu.VMEM((tm, D_MODEL), BF16)],
        compiler_params=_cparams(("parallel", "arbitrary")),
        name="in_proj",
    )(s, mod, ng, w)


def _prep_kernel(x_ref, prev_ref, next_ref, small_ref, cw_ref, cb_ref, alog_ref, dtb_ref,
                 cvg_ref, cvs_ref, gb_ref, dt_ref, gt_ref, at_ref, ext_scr, *, nx_tiles,
                 x_tiles_per_seq, c_tiles_per_seq):
    i = pl.program_id(0)
    t = x_ref.shape[0]
    pos = jnp.where(i < nx_tiles, i % x_tiles_per_seq, (i - nx_tiles) % c_tiles_per_seq)
    last = jnp.where(i < nx_tiles, x_tiles_per_seq - 1, c_tiles_per_seq - 1)
    has_prev = (pos != 0).astype(F32)
    has_next = (pos != last).astype(F32)
    pad = CONV_K // 2

    ncol = CONV_W // 512
    for cc in range(ncol):
        cs = slice(cc * 512, (cc + 1) * 512)
        ext_scr[0:8, :] = prev_ref[:, cs] * has_prev
        ext_scr[8:8 + t, :] = x_ref[:, cs]
        ext_scr[8 + t:16 + t, :] = next_ref[:, cs] * has_next
        y = jnp.broadcast_to(cb_ref[:, cs], (t, 512))
        for k in range(CONV_K):
            off = 8 - pad + k
            y = y + cw_ref[k:k + 1, cs] * ext_scr[off:off + t, :]
        y = _silu(y)
        if cc < 2:
            for h in range(GDN_HEADS):
                hs = slice(h * GDN_DK, (h + 1) * GDN_DK)
                blk = y[:, hs]
                n = blk * lax.rsqrt(jnp.sum(blk * blk, axis=-1, keepdims=True) + EPS)
                if cc == 0:
                    n = n * (GDN_DK ** -0.5)
                cvg_ref[:, cc * 512 + h * GDN_DK:cc * 512 + (h + 1) * GDN_DK] = n
        elif cc == 2:
            cvg_ref[:, cs] = y
        else:
            cvs_ref[:, (cc - 3) * 512:(cc - 2) * 512] = y

    sm = small_ref[...]
    lane = lax.broadcasted_iota(jnp.int32, (t, 128), 1)
    row = lax.broadcasted_iota(jnp.int32, (t, 128), 0)
    z = sm + dtb_ref[...]
    sp = jnp.maximum(z, 0.0) + jnp.log1p(jnp.exp(-jnp.abs(z)))
    g = -jnp.exp(alog_ref[...]) * sp
    is_gdn = lane < LANE_GB
    is_ssd = (lane >= LANE_DT) & (lane < LANE_DT + 2 * SSD_HEADS)
    win = jnp.where(is_gdn, GDN_CHUNK, SSD_CHUNK)
    rmod = row % win
    g = jnp.where(is_gdn | is_ssd, g, 0.0)
    pre, suf = g, g
    sft = 1
    while sft < SSD_CHUNK:
        pre = pre + jnp.where(rmod >= sft, pltpu.roll(pre, sft, 0), 0.0)
        suf = suf + jnp.where(rmod < win - sft, pltpu.roll(suf, t - sft, 0), 0.0)
        sft *= 2
    backward = (is_gdn & (lane >= GDN_HEADS)) | (is_ssd & (lane >= LANE_DT + SSD_HEADS))
    cum = jnp.where(backward, suf, pre)
    is_beta = (lane >= LANE_GB) & (lane < LANE_DT)
    gb = jnp.where(is_beta, jax.nn.sigmoid(sm), cum)
    gb_ref[...] = gb
    dt_ref[...] = sp
    gbt = gb.T
    for c in range(t // GDN_CHUNK):
        gt_ref[c] = gbt[0:16, c * GDN_CHUNK:(c + 1) * GDN_CHUNK]
    for c in range(t // SSD_CHUNK):
        at_ref[c] = gbt[16:32, c * SSD_CHUNK:(c + 1) * SSD_CHUNK]


def _prep(f, cw, cb, alog, dtb, *, n_tiles, nx_tiles, x_tiles_per_seq, c_tiles_per_seq):
    t = SEQ_T
    rows = n_tiles * t
    r8 = rows // 8
    t8 = t // 8
    return pl.pallas_call(
        functools.partial(_prep_kernel, nx_tiles=nx_tiles, x_tiles_per_seq=x_tiles_per_seq,
                          c_tiles_per_seq=c_tiles_per_seq),
        grid=(n_tiles,),
        in_specs=[
            pl.BlockSpec((t, CONV_W), lambda i: (i, 0)),
            pl.BlockSpec((8, CONV_W), lambda i: (jnp.maximum(i * t8 - 1, 0), 0)),
            pl.BlockSpec((8, CONV_W), lambda i: (jnp.minimum((i + 1) * t8, r8 - 1), 0)),
            pl.BlockSpec((t, 128), lambda i: (i, F_SMALL // 128)),
            _const_spec(cw.shape),
            _const_spec(cb.shape),
            _const_spec(alog.shape),
            _const_spec(dtb.shape),
        ],
        out_specs=[
            pl.BlockSpec((t, GDN_QKV), lambda i: (i, 0)),
            pl.BlockSpec((t, SSD_XBC), lambda i: (i, 0)),
            pl.BlockSpec((t, 128), lambda i: (i, 0)),
            pl.BlockSpec((t, 128), lambda i: (i, 0)),
            pl.BlockSpec((t // GDN_CHUNK, 16, GDN_CHUNK), lambda i: (i, 0, 0)),
            pl.BlockSpec((t // SSD_CHUNK, 16, SSD_CHUNK), lambda i: (i, 0, 0)),
        ],
        out_shape=[
            jax.ShapeDtypeStruct((rows, GDN_QKV), F32),
            jax.ShapeDtypeStruct((rows, SSD_XBC), F32),
            jax.ShapeDtypeStruct((rows, 128), F32),
            jax.ShapeDtypeStruct((rows, 128), F32),
            jax.ShapeDtypeStruct((rows // GDN_CHUNK, 16, GDN_CHUNK), F32),
            jax.ShapeDtypeStruct((rows // SSD_CHUNK, 16, SSD_CHUNK), F32),
        ],
        scratch_shapes=[pltpu.VMEM((t + 16, 512), F32)],
        compiler_params=_cparams(("parallel",)),
        name="conv_prep",
    )(f, f, f, f, cw, cb, alog, dtb)


def _rope_kernel(x_ref, cos_ref, sinp_ref, sinm_ref, o_ref, *, nx_tiles):
    i = pl.program_id(0)
    is_x = i < nx_tiles
    cos, sinp, sinm = cos_ref[...], sinp_ref[...], sinm_ref[...]
    nblk = x_ref.shape[1] // 128
    for b in range(nblk):
        bs = slice(b * 128, (b + 1) * 128)
        v = x_ref[:, bs]
        r = v * cos + pltpu.roll(v, 32, 1) * sinp + pltpu.roll(v, 96, 1) * sinm
        r = jnp.where(is_x, r, v)
        if b < nblk // 2:
            r = r * (DIFF_DQK ** -0.5)
        o_ref[:, bs] = r.astype(BF16)


def _rope(f, cos, sinp, sinm, *, n_tiles, nx_tiles, x_tiles_per_seq):
    t = SEQ_T
    tab = pl.BlockSpec((t, 128), lambda i: (jnp.where(i < nx_tiles, i % x_tiles_per_seq, 0), 0))
    return pl.pallas_call(
        functools.partial(_rope_kernel, nx_tiles=nx_tiles),
        grid=(n_tiles,),
        in_specs=[pl.BlockSpec((t, 1024), lambda i: (i, F_DQ // 1024)), tab, tab, tab],
        out_specs=pl.BlockSpec((t, 1024), lambda i: (i, 0)),
        out_shape=jax.ShapeDtypeStruct((n_tiles * t, 1024), BF16),
        compiler_params=_cparams(("parallel",)),
        name="rope",
    )(f, cos, sinp, sinm)


def _attn_kernel(*refs, tk, n_kx, lam_init, with_x):
    if with_x:
        lam_ref, nw_ref, q_ref, kx_ref, vx_ref, kc_ref, vc_ref, o_ref, acc0, acc1 = refs
    else:
        lam_ref, nw_ref, q_ref, kc_ref, vc_ref, _, o_ref, acc0, acc1 = refs
    q = q_ref[...]
    tq = q.shape[0]
    lane = lax.broadcasted_iota(jnp.int32, q.shape, 1)
    zero = jnp.zeros_like(q)
    qh = (jnp.where(lane < DIFF_DQK, q, zero), jnp.where(lane >= DIFF_DQK, q, zero))
    accs = (acc0, acc1)
    acc0[...] = jnp.zeros_like(acc0)
    acc1[...] = jnp.zeros_like(acc1)

    def tile(k, v, carry):
        out = []
        for j in range(2):
            m_old, l_old = carry[2 * j], carry[2 * j + 1]
            s = lax.dot_general(qh[j], k, (((1,), (1,)), ((), ())), preferred_element_type=F32)
            m_new = jnp.maximum(m_old, jnp.max(s, axis=-1, keepdims=True))
            alpha = jnp.exp(m_old - m_new)
            p = jnp.exp(s - m_new)
            l_new = alpha * l_old + jnp.sum(p, axis=-1, keepdims=True)
            accs[j][...] = alpha * accs[j][...] + jnp.dot(p.astype(BF16), v,
                                                         preferred_element_type=F32)
            out += [m_new, l_new]
        return tuple(out)

    ninf = jnp.full((tq, 1), -jnp.inf, F32)
    zl = jnp.zeros((tq, 1), F32)
    carry = (ninf, zl, ninf, zl)
    if with_x:
        def body(t, c):
            r0 = pl.multiple_of(t * tk, tk)
            return tile(kx_ref[pl.ds(r0, tk), :], vx_ref[pl.ds(r0, tk), :], c)
        carry = lax.fori_loop(0, n_kx, body, carry)
    carry = tile(kc_ref[...], vc_ref[...], carry)

    lam = lam_ref[...]
    lam_full = (jnp.exp(jnp.sum(lam[0:1] * lam[1:2], axis=-1, keepdims=True))
                - jnp.exp(jnp.sum(lam[2:3] * lam[3:4], axis=-1, keepdims=True)) + lam_init)
    o = acc0[...] / carry[1] - lam_full * (acc1[...] / carry[3])
    o_ref[...] = (_rms(o) * nw_ref[...] * (1.0 - lam_init)).astype(BF16)


def _attention(qk, hh, lam, nw, *, bsz, seq, ctx_len, lam_init, with_ctx):
    tq, tk = 512, 512
    rows = bsz * (seq + ctx_len)
    nq = seq // tq
    cblk = bsz * seq // ctx_len
    out = pl.pallas_call(
        functools.partial(_attn_kernel, tk=tk, n_kx=seq // tk, lam_init=lam_init, with_x=True),
        grid=(bsz, DIFF_HEADS, nq),
        in_specs=[
            _const_spec(lam.shape),
            _const_spec(nw.shape),
            pl.BlockSpec((tq, 128), lambda b, h, i: (b * nq + i, h)),
            pl.BlockSpec((seq, 128), lambda b, h, i: (b, DIFF_HEADS + h)),
            pl.BlockSpec((seq, 128), lambda b, h, i: (b, H_DV // 128 + h)),
            pl.BlockSpec((ctx_len, 128), lambda b, h, i: (cblk + b, DIFF_HEADS + h)),
            pl.BlockSpec((ctx_len, 128), lambda b, h, i: (cblk + b, H_DV // 128 + h)),
        ],
        out_specs=pl.BlockSpec((tq, 128), lambda b, h, i: (b * nq + i, h)),
        out_shape=jax.ShapeDtypeStruct((rows, DIFF_HEADS * DIFF_DV), BF16),
        scratch_shapes=[pltpu.VMEM((tq, 128), F32), pltpu.VMEM((tq, 128), F32)],
        compiler_params=_cparams(("parallel", "parallel", "arbitrary")),
        name="diff_attn",
    )(lam, nw, qk, qk, hh, qk, hh)
    if not with_ctx:
        return out
    return pl.pallas_call(
        functools.partial(_attn_kernel, tk=tk, n_kx=0, lam_init=lam_init, with_x=False),
        grid=(bsz, DIFF_HEADS),
        in_specs=[
            _const_spec(lam.shape),
            _const_spec(nw.shape),
            pl.BlockSpec((ctx_len, 128), lambda b, h: (cblk + b, h)),
            pl.BlockSpec((ctx_len, 128), lambda b, h: (cblk + b, DIFF_HEADS + h)),
            pl.BlockSpec((ctx_len, 128), lambda b, h: (cblk + b, H_DV // 128 + h)),
            pl.BlockSpec(memory_space=pl.ANY),
        ],
        out_specs=pl.BlockSpec((ctx_len, 128), lambda b, h: (cblk + b, h)),
        out_shape=jax.ShapeDtypeStruct((rows, DIFF_HEADS * DIFF_DV), BF16),
        scratch_shapes=[pltpu.VMEM((ctx_len, 128), F32), pltpu.VMEM((ctx_len, 128), F32)],
        input_output_aliases={5: 0},
        compiler_params=_cparams(("parallel", "parallel")),
        name="diff_attn_ctx",
    )(lam, nw, qk, qk, hh, out)


def _seq_maps(bsz, nsx, nsc):
    cbase = bsz * nsx

    def fwd(b, s):
        return jnp.where(s < nsc, cbase + b * nsc + s, b * nsx + (s - nsc))

    def bwd(b, s):
        return jnp.where(s < nsc, cbase + b * nsc + (nsc - 1 - s), b * nsx + (nsx - 1 - (s - nsc)))

    return fwd, bwd


def _gdn_kernel(cvf_ref, gbf_ref, gtf_ref, cvb_ref, gbb_ref, gtb_ref, of_ref, ob_ref, st_ref):
    @pl.when(pl.program_id(1) == 0)
    def _():
        st_ref[...] = jnp.zeros_like(st_ref)

    c_ = GDN_CHUNK
    nck = cvf_ref.shape[0] // c_
    ii = lax.broadcasted_iota(jnp.int32, (c_, c_), 0)
    jj = lax.broadcasted_iota(jnp.int32, (c_, c_), 1)
    eye = (ii == jj).astype(F32)
    blk16 = (ii // 16) == (jj // 16)
    same32 = (ii // 32) == (jj // 32)
    lvl32 = same32 & jnp.logical_not(blk16)
    lvl64 = jnp.logical_not(same32)
    dirs = (
        (cvf_ref, gbf_ref, gtf_ref, of_ref, ii >= jj, ii > jj, c_ - 1),
        (cvb_ref, gbb_ref, gtb_ref, ob_ref, ii <= jj, ii < jj, 0),
    )

    def body(c, _):
        for d, (cv_ref, gb_ref, gt_ref, o_ref, incl, strict, last_row) in enumerate(dirs):
            cc = c if d == 0 else nck - 1 - c
            r0 = pl.multiple_of(cc * c_, c_)
            rows = pl.ds(r0, c_)
            gbt = gb_ref[rows, :]
            gt = gt_ref[cc]
            for h in range(GDN_HEADS):
                idx = d * GDN_HEADS + h
                hs = slice(h * GDN_DK, (h + 1) * GDN_DK)
                qh = cv_ref[rows, hs]
                kh = cv_ref[rows, 512 + h * GDN_DK:512 + (h + 1) * GDN_DK]
                vh = cv_ref[rows, 1024 + h * GDN_DV:1024 + (h + 1) * GDN_DV]
                gcol = gbt[:, LANE_GA + idx:LANE_GA + idx + 1]
                beta = gbt[:, LANE_GB + idx:LANE_GB + idx + 1]
                grow = gt[idx:idx + 1, :]
                glast = gbt[last_row:last_row + 1, LANE_GA + idx:LANE_GA + idx + 1]
                dec = jnp.where(incl, jnp.exp(jnp.minimum(gcol - grow, 0.0)), 0.0)
                eg = jnp.exp(gcol)
                kbeta = kh * beta
                ga = _bdot_nt(jnp.concatenate([kbeta, qh], axis=0), kh)
                m = jnp.where(strict, ga[:c_] * dec, 0.0)
                aqk = ga[c_:] * dec
                mb = jnp.where(blk16, m, 0.0)
                tinv = eye - mb
                sq = _bdot(mb, mb)
                tinv = tinv + _bdot(tinv, sq)
                sq = _bdot(sq, sq)
                tinv = tinv + _bdot(tinv, sq)
                sq = _bdot(sq, sq)
                tinv = tinv + _bdot(tinv, sq)
                for lvl in (lvl32, lvl64):
                    off = jnp.where(lvl, m, 0.0)
                    tinv = tinv - _bdot(tinv, _bdot(off, tinv))
                uw = _bdot(tinv, jnp.concatenate([vh * beta, kbeta * eg], axis=1))
                s_old = st_ref[d, h]
                wq = _bdot(jnp.concatenate([uw[:, GDN_DV:], qh * eg], axis=0), s_old)
                v_new = uw[:, :GDN_DV] - wq[:c_]
                o_ref[rows, hs] = wq[c_:] + _bdot(aqk, v_new)
                k_dec = kh * jnp.exp(glast - gcol)
                st_ref[d, h] = s_old * jnp.exp(glast) + _bdot_tn(k_dec, v_new)
        return 0

    lax.fori_loop(0, nck, body, 0)


def _gdn(cv, gb, gt, *, bsz, nsx, nsc):
    t = SEQ_T
    rows = cv.shape[0]
    fwd, bwd = _seq_maps(bsz, nsx, nsc)
    nck = t // GDN_CHUNK

    def specs(fn):
        return [
            pl.BlockSpec((t, GDN_QKV), lambda b, s: (fn(b, s), 0)),
            pl.BlockSpec((t, 128), lambda b, s: (fn(b, s), 0)),
            pl.BlockSpec((nck, 16, GDN_CHUNK), lambda b, s: (fn(b, s), 0, 0)),
        ]

    return pl.pallas_call(
        _gdn_kernel,
        grid=(bsz, nsc + nsx),
        in_specs=specs(fwd) + specs(bwd),
        out_specs=[pl.BlockSpec((t, 512), lambda b, s: (fwd(b, s), 0)),
                   pl.BlockSpec((t, 512), lambda b, s: (bwd(b, s), 0))],
        out_shape=[jax.ShapeDtypeStruct((rows, 512), F32)] * 2,
        scratch_shapes=[pltpu.VMEM((2, GDN_HEADS, GDN_DK, GDN_DV), F32)],
        compiler_params=_cparams(("parallel", "arbitrary")),
        name="gdn_scan",
    )(cv, gb, gt, cv, gb, gt)


def _ssd_kernel(dsk_ref, cvf_ref, gbf_ref, dtf_ref, atf_ref, cvb_ref, gbb_ref, dtb_ref, atb_ref,
                yf_ref, yb_ref, st_ref):
    @pl.when(pl.program_id(1) == 0)
    def _():
        st_ref[...] = jnp.zeros_like(st_ref)

    c_ = SSD_CHUNK
    p_ = SSD_HEADDIM
    hpg = SSD_HEADS // SSD_GROUPS
    nck = cvf_ref.shape[0] // c_
    ii = lax.broadcasted_iota(jnp.int32, (c_, c_), 0)
    jj = lax.broadcasted_iota(jnp.int32, (c_, c_), 1)
    dirs = (
        (cvf_ref, gbf_ref, dtf_ref, atf_ref, yf_ref, ii >= jj, c_ - 1),
        (cvb_ref, gbb_ref, dtb_ref, atb_ref, yb_ref, ii <= jj, 0),
    )

    def body(c, _):
        for d, (cv_ref, gb_ref, dt_ref, at_ref, y_ref, incl, last_row) in enumerate(dirs):
            cc = c if d == 0 else nck - 1 - c
            r0 = pl.multiple_of(cc * c_, c_)
            rows = pl.ds(r0, c_)
            acs = gb_ref[rows, :]
            dts = dt_ref[rows, :]
            at = at_ref[cc]
            for g in range(SSD_GROUPS):
                bm = cv_ref[rows, SSD_INNER + g * SSD_STATE:SSD_INNER + (g + 1) * SSD_STATE]
                cm = cv_ref[rows, SSD_INNER + (SSD_GROUPS + g) * SSD_STATE:
                            SSD_INNER + (SSD_GROUPS + g + 1) * SSD_STATE]
                cb = _bdot_nt(cm, bm)
                s_old = st_ref[d, g]
                y_off = _bdot(cm, s_old)
                xd, sdec = [], []
                for hh in range(hpg):
                    h = g * hpg + hh
                    idx = d * SSD_HEADS + h
                    ln = LANE_DT + idx
                    acol = acs[:, ln:ln + 1]
                    arow = at[idx:idx + 1, :]
                    alast = acs[last_row:last_row + 1, ln:ln + 1]
                    xh = cv_ref[rows, h * p_:(h + 1) * p_]
                    xdt = xh * dts[:, ln:ln + 1]
                    ldec = jnp.where(incl, jnp.exp(jnp.minimum(acol - arow, 0.0)), 0.0)
                    y = _bdot(cb * ldec, xdt) + y_off[:, hh * p_:(hh + 1) * p_] * jnp.exp(acol)
                    if d == 0:
                        y = y + dsk_ref[:, h * p_:(h + 1) * p_] * xh
                    y_ref[rows, h * p_:(h + 1) * p_] = y
                    xd.append(xdt * jnp.exp(alast - acol))
                    sdec.append(jnp.broadcast_to(jnp.exp(alast), (1, p_)))
                st_ref[d, g] = (s_old * jnp.concatenate(sdec, axis=1)
                                + _bdot_tn(bm, jnp.concatenate(xd, axis=1)))
        return 0

    lax.fori_loop(0, nck, body, 0)


def _ssd(dsk, cv, gb, dt, at, *, bsz, nsx, nsc):
    t = SEQ_T
    rows = cv.shape[0]
    fwd, bwd = _seq_maps(bsz, nsx, nsc)
    nck = t // SSD_CHUNK

    def specs(fn):
        return [
            pl.BlockSpec((t, SSD_XBC), lambda b, s: (fn(b, s), 0)),
            pl.BlockSpec((t, 128), lambda b, s: (fn(b, s), 0)),
            pl.BlockSpec((t, 128), lambda b, s: (fn(b, s), 0)),
            pl.BlockSpec((nck, 16, SSD_CHUNK), lambda b, s: (fn(b, s), 0, 0)),
        ]

    return pl.pallas_call(
        _ssd_kernel,
        grid=(bsz, nsc + nsx),
        in_specs=[_const_spec(dsk.shape)] + specs(fwd) + specs(bwd),
        out_specs=[pl.BlockSpec((t, SSD_INNER), lambda b, s: (fwd(b, s), 0)),
                   pl.BlockSpec((t, SSD_INNER), lambda b, s: (bwd(b, s), 0))],
        out_shape=[jax.ShapeDtypeStruct((rows, SSD_INNER), F32)] * 2,
        scratch_shapes=[pltpu.VMEM((2, SSD_GROUPS, SSD_STATE, SSD_INNER // SSD_GROUPS), F32)],
        compiler_params=_cparams(("parallel", "arbitrary")),
        name="ssd_scan",
    )(dsk, cv, gb, dt, at, cv, gb, dt, at)


def _merge_kernel(x_ref, mod_ref, ng_ref, gof_ref, gob_ref, gz_ref, att_ref, syf_ref, syb_ref,
                  sz_ref, gate_ref, gnw_ref, snw_ref, wb_ref, wo_ref, o_ref):
    oa = gof_ref[...] + gob_ref[...]
    parts = []
    for h in range(GDN_HEADS):
        hs = slice(h * GDN_DV, (h + 1) * GDN_DV)
        parts.append(_rms(oa[:, hs]) * gnw_ref[:, hs])
    oa = jnp.concatenate(parts, axis=1) * _silu(gz_ref[...].astype(F32))
    oc = (syf_ref[...] + syb_ref[...]) * _silu(sz_ref[...].astype(F32))
    oc = _rms(oc) * snw_ref[...]
    outs = (oa.astype(BF16), att_ref[...], oc.astype(BF16))
    y = None
    for n in range(N_BRANCH):
        gate = jax.nn.sigmoid(gate_ref[:, n * D_MODEL:(n + 1) * D_MODEL].astype(F32))
        term = gate * jnp.dot(outs[n], wb_ref[n], preferred_element_type=F32)
        y = term if y is None else y + term
    y = jnp.dot(y.astype(BF16), wo_ref[...], preferred_element_type=F32)
    m = mod_ref[0]
    o_ref[...] = x_ref[...] + m[5:6] * (_rms(y) * ng_ref[3:4, :])


def _merge(s, mod, ng, gof, gob, hh, att, syf, syb, gnw, snw, wb, wo, *, tm, n_tiles,
           tiles_per_sample, n_samples):
    rows = n_tiles * tm
    row512 = pl.BlockSpec((tm, 512), lambda i: (i, 0))
    return pl.pallas_call(
        _merge_kernel,
        grid=(n_tiles,),
        in_specs=[
            pl.BlockSpec((tm, D_MODEL), lambda i: (i, 0)),
            pl.BlockSpec((1, N_MOD, D_MODEL),
                         lambda i: (jnp.minimum(i // tiles_per_sample, n_samples), 0, 0)),
            _const_spec(ng.shape),
            row512, row512,
            pl.BlockSpec((tm, 512), lambda i: (i, H_GZ // 512)),
            row512, row512, row512,
            pl.BlockSpec((tm, 512), lambda i: (i, H_SZ // 512)),
            pl.BlockSpec((tm, N_BRANCH * D_MODEL), lambda i: (i, 0)),
            _const_spec(gnw.shape),
            _const_spec(snw.shape),
            _const_spec(wb.shape),
            _const_spec(wo.shape),
        ],
        out_specs=pl.BlockSpec((tm, D_MODEL), lambda i: (i, 0)),
        out_shape=jax.ShapeDtypeStruct((rows, D_MODEL), F32),
        compiler_params=_cparams(("parallel",)),
        name="merge",
    )(s, mod, ng, gof, gob, hh, att, syf, syb, hh, hh, gnw, snw, wb, wo)


def _deinterleave_pairs(w):
    d = w.shape[0]
    w = w.reshape(d, DIFF_HEADS, 2, DIFF_DQK // 2, 2)
    return jnp.swapaxes(w, 3, 4).reshape(d, DIFF_HEADS * 2 * DIFF_DQK)


def _proj_weight(w_in):
    o = 0
    parts = {}
    for name, width in (("gqkv", GDN_QKV), ("gz", 512), ("ga", 8), ("gb", 8), ("dq", 512),
                        ("dk", 512), ("dv", 512), ("sz", 512), ("sxbc", SSD_XBC), ("sdt", 16),
                        ("gate", N_BRANCH * D_MODEL)):
        parts[name] = w_in[:, o:o + width]
        o += width
    small_pad = jnp.zeros((w_in.shape[0], SMALL_W - 32), w_in.dtype)
    cols = [parts["gqkv"], parts["sxbc"], parts["ga"], parts["gb"], parts["sdt"], small_pad,
            _deinterleave_pairs(parts["dq"]), _deinterleave_pairs(parts["dk"]),
            parts["gate"], parts["dv"], parts["gz"], parts["sz"]]
    return jnp.concatenate(cols, axis=1).astype(BF16)


def _rope_tables(seq):
    rows = seq // GRID_W
    row = jnp.repeat(jnp.arange(rows, dtype=F32), GRID_W)
    col = jnp.tile(jnp.arange(GRID_W, dtype=F32), rows)
    inv = ROPE_BASE ** (-jnp.arange(ROPE_PAIRS_PER_AXIS, dtype=F32) / ROPE_PAIRS_PER_AXIS)
    ang = jnp.concatenate([row[:, None] * inv, col[:, None] * inv], axis=-1)
    cos, sin = jnp.cos(ang), jnp.sin(ang)
    zero = jnp.zeros_like(sin)
    cos = jnp.tile(cos, (1, 4))
    sinp = jnp.tile(jnp.concatenate([zero, sin], axis=1), (1, 2))
    sinm = jnp.tile(jnp.concatenate([-sin, zero], axis=1), (1, 2))
    return cos, sinp, sinm


def _lane_vec(pieces):
    v = jnp.zeros((128,), F32)
    for off, val in pieces:
        v = lax.dynamic_update_slice(v, val.reshape(-1).astype(F32), (off,))
    return v.reshape(1, 128)


def kernel(x, c, ctx, c_ctx, w_ada, b_ada, norm_g, w_ffn_in, w_ffn_out, w_in, gdn_conv_w, gdn_a_log,
           gdn_dt_bias, gdn_norm_w, diff_lambda, diff_norm_w, ssd_conv_w, ssd_conv_b, ssd_a_log,
           ssd_dt_bias, ssd_d, ssd_norm_w, w_branch, w_out):
    bsz, seq, _ = x.shape
    ctx_len = ctx.shape[1]
    t = SEQ_T
    tm = 512
    assert ctx_len == t and seq % 512 == 0 and (bsz * ctx_len) % tm == 0 and seq % GRID_W == 0
    nx_rows, nc_rows = bsz * seq, bsz * ctx_len
    n_rows = nx_rows + nc_rows

    s = jnp.concatenate([x.reshape(nx_rows, D_MODEL), ctx.reshape(nc_rows, D_MODEL)], axis=0)
    nm = -(-(bsz + 1) // 8) * 8
    cvec = jnp.zeros((nm, D_MODEL), F32).at[:bsz].set(c).at[bsz].set(c_ctx)
    mod_all = _mod_table(cvec, w_ada, b_ada)
    cos, sinp, sinm = _rope_tables(seq)

    tok = dict(tm=tm, tiles_per_sample=seq // tm, n_samples=bsz)
    all_tiles, x_tiles = n_rows // tm, nx_rows // tm
    seq_kw = dict(n_tiles=n_rows // t, nx_tiles=nx_rows // t, x_tiles_per_seq=seq // t)
    nsx, nsc = seq // t, ctx_len // t

    for i in range(DEPTH):
        last = i == DEPTH - 1
        lam_init = 0.8 - 0.6 * math.exp(-0.3 * i)
        mod, ng = mod_all[i], norm_g[i]
        w1 = [w_ffn_in[i, j].astype(BF16) for j in range(2)]
        w2 = [w_ffn_out[i, j].astype(BF16) for j in range(2)]

        s = _half_ffn(s, mod, ng, w1[0], w2[0], slot=0, gslot=0, n_tiles=all_tiles, **tok)

        f, hh = _in_proj(s, mod, ng, _proj_weight(w_in[i]), n_tiles=all_tiles, **tok)
        cw = jnp.concatenate([gdn_conv_w[i], ssd_conv_w[i]], axis=1)
        cb = jnp.concatenate([jnp.zeros((GDN_QKV,), F32), ssd_conv_b[i]]).reshape(1, CONV_W)
        alog = _lane_vec([(LANE_GA, gdn_a_log[i]), (LANE_DT, ssd_a_log[i])])
        dtb = _lane_vec([(LANE_GA, gdn_dt_bias[i]), (LANE_DT, ssd_dt_bias[i])])
        cvg, cvs, gb, dt, gt, at = _prep(f, cw, cb, alog, dtb, c_tiles_per_seq=ctx_len // t, **seq_kw)
        qk = _rope(f, cos, sinp, sinm, n_tiles=seq_kw["n_tiles"], nx_tiles=seq_kw["nx_tiles"],
                   x_tiles_per_seq=seq_kw["x_tiles_per_seq"])

        gof, gob = _gdn(cvg, gb, gt, bsz=bsz, nsx=nsx, nsc=nsc)
        dsk = jnp.repeat(ssd_d[i].astype(F32), SSD_HEADDIM).reshape(1, SSD_INNER)
        syf, syb = _ssd(dsk, cvs, gb, dt, at, bsz=bsz, nsx=nsx, nsc=nsc)
        att = _attention(qk, hh, diff_lambda[i], diff_norm_w[i].reshape(1, DIFF_DV), bsz=bsz,
                         seq=seq, ctx_len=ctx_len, lam_init=lam_init, with_ctx=not last)

        n_out = x_tiles if last else all_tiles
        gnw = jnp.tile(gdn_norm_w[i], GDN_HEADS).reshape(1, GDN_HEADS * GDN_DV)
        snw = ssd_norm_w[i].reshape(1, SSD_INNER)
        s = _merge(s, mod, ng, gof, gob, hh, att, syf, syb, gnw, snw, w_branch[i].astype(BF16),
                   w_out[i].astype(BF16), n_tiles=n_out, **tok)
        s = _half_ffn(s, mod, ng, w1[1], w2[1], slot=6, gslot=4, n_tiles=n_out, **tok)

    return s[:nx_rows].reshape(bsz, seq, D_MODEL)
```

```python
import functools
import math

import jax
import jax.numpy as jnp
from jax import lax
from jax.experimental import pallas as pl
from jax.experimental.pallas import tpu as pltpu

F32 = jnp.float32
BF16 = jnp.bfloat16

D_MODEL = 1024
DEPTH = 2
GRID_W = 64
EPS = 1e-6
N_MOD = 9
D_FF = 2816
CONV_K = 5
GDN_HEADS = 4
GDN_DK = 128
GDN_DV = 128
GDN_CHUNK = 64
DIFF_HEADS = 4
DIFF_DQK = 64
DIFF_DV = 128
ROPE_BASE = 10000.0
ROPE_PAIRS_PER_AXIS = DIFF_DQK // 4
SSD_HEADS = 8
SSD_HEADDIM = 64
SSD_INNER = SSD_HEADS * SSD_HEADDIM
SSD_GROUPS = 2
SSD_STATE = 128
SSD_CHUNK = 128
N_BRANCH = 3
BRANCH_W = 512

GDN_QKV = 2 * GDN_HEADS * GDN_DK + GDN_HEADS * GDN_DV
SSD_XBC = SSD_INNER + 2 * SSD_GROUPS * SSD_STATE
CONV_W = GDN_QKV + SSD_XBC
SMALL_W = 512
F_SMALL = CONV_W
F_DQ = F_SMALL + SMALL_W
F_W = F_DQ + 2 * 512
H_DV = N_BRANCH * D_MODEL
H_GZ = H_DV + 512
H_SZ = H_GZ + 512
H_W = H_SZ + 512
LANE_GA = 0
LANE_GB = 8
LANE_DT = 16

Q_SCALE = DIFF_DQK ** -0.5 * math.log2(math.e)

FF_CHUNK = 256
PROJ_TN = 512
SEQ_T = 256
VMEM_LIMIT = 56 * 1024 * 1024


def _cparams(sem):
    return pltpu.CompilerParams(dimension_semantics=sem, vmem_limit_bytes=VMEM_LIMIT)


def _const_spec(shape):
    nd = len(shape)
    return pl.BlockSpec(shape, lambda *_: (0,) * nd, pipeline_mode=pl.Buffered(1))


def _silu(v):
    return v * jax.nn.sigmoid(v)


def _bdot(a, b):
    return jnp.dot(a.astype(BF16), b.astype(BF16), preferred_element_type=F32)


def _bdot_nt(a, b):
    return lax.dot_general(a.astype(BF16), b.astype(BF16), (((1,), (1,)), ((), ())),
                           preferred_element_type=F32)


def _bdot_tn(a, b):
    return lax.dot_general(a.astype(BF16), b.astype(BF16), (((0,), (0,)), ((), ())),
                           preferred_element_type=F32)


def _rms(v):
    return v * lax.rsqrt(jnp.mean(v * v, axis=-1, keepdims=True) + EPS)


def _mod_kernel(c_ref, w_ref, b_ref, o_ref):
    sc = _silu(c_ref[...])
    o_ref[0, 0] = jnp.dot(sc, w_ref[0], precision=lax.Precision.HIGHEST,
                          preferred_element_type=F32) + b_ref[0]


def _mod_table(cvec, w_ada, b_ada):
    nm = cvec.shape[0]
    out = pl.pallas_call(
        _mod_kernel,
        grid=(DEPTH, N_MOD),
        in_specs=[
            pl.BlockSpec((nm, D_MODEL), lambda l, j: (0, 0)),
            pl.BlockSpec((1, D_MODEL, D_MODEL), lambda l, j: (l, 0, j)),
            pl.BlockSpec((1, 1, D_MODEL), lambda l, j: (l, 0, j)),
        ],
        out_specs=pl.BlockSpec((1, 1, nm, D_MODEL), lambda l, j: (l, j, 0, 0)),
        out_shape=jax.ShapeDtypeStruct((DEPTH, N_MOD, nm, D_MODEL), F32),
        compiler_params=_cparams(("parallel", "parallel")),
        name="mod_table",
    )(cvec, w_ada, b_ada.reshape(DEPTH, 1, N_MOD * D_MODEL))
    return jnp.transpose(out, (0, 2, 1, 3))


def _ffn_kernel(x_ref, mod_ref, ng_ref, w1_ref, w2_ref, o_ref, *, slot, gslot):
    x = x_ref[...]
    m = mod_ref[0]
    shift, scale, gate = m[slot:slot + 1], m[slot + 1:slot + 2], m[slot + 2:slot + 3]
    h = _rms(x) * ng_ref[gslot:gslot + 1, :]
    hb = (h * (1.0 + scale) + shift).astype(BF16)
    acc = None
    for c in range(D_FF // FF_CHUNK):
        lo = c * FF_CHUNK
        g = jnp.dot(hb, w1_ref[:, lo:lo + FF_CHUNK], preferred_element_type=F32)
        u = jnp.dot(hb, w1_ref[:, D_FF + lo:D_FF + lo + FF_CHUNK], preferred_element_type=F32)
        a = (_silu(g) * u).astype(BF16)
        part = jnp.dot(a, w2_ref[lo:lo + FF_CHUNK, :], preferred_element_type=F32)
        acc = part if acc is None else acc + part
    y = _rms(acc) * ng_ref[gslot + 1:gslot + 2, :]
    o_ref[...] = x + 0.5 * gate * y


def _half_ffn(s, mod, ng, w1, w2, *, slot, gslot, tm, n_tiles, tiles_per_sample, n_samples):
    rows = n_tiles * tm
    return pl.pallas_call(
        functools.partial(_ffn_kernel, slot=slot, gslot=gslot),
        grid=(n_tiles,),
        in_specs=[
            pl.BlockSpec((tm, D_MODEL), lambda i: (i, 0)),
            pl.BlockSpec((1, N_MOD, D_MODEL),
                         lambda i: (jnp.minimum(i // tiles_per_sample, n_samples), 0, 0)),
            _const_spec(ng.shape),
            _const_spec(w1.shape),
            _const_spec(w2.shape),
        ],
        out_specs=pl.BlockSpec((tm, D_MODEL), lambda i: (i, 0)),
        out_shape=jax.ShapeDtypeStruct((rows, D_MODEL), F32),
        compiler_params=_cparams(("parallel",)),
        name="half_ffn",
    )(s, mod, ng, w1, w2)


def _inproj_kernel(x_ref, mod_ref, ng_ref, w_ref, of_ref, oh_ref):
    m = mod_ref[0]
    h = _rms(x_ref[...]) * ng_ref[2:3, :]
    hb = (h * (1.0 + m[4:5]) + m[3:4]).astype(BF16)
    for j in range(F_W // PROJ_TN):
        cs = slice(j * PROJ_TN, (j + 1) * PROJ_TN)
        of_ref[:, cs] = jnp.dot(hb, w_ref[:, cs], preferred_element_type=F32)
    for j in range(H_W // PROJ_TN):
        cs = slice(j * PROJ_TN, (j + 1) * PROJ_TN)
        ws = slice(F_W + j * PROJ_TN, F_W + (j + 1) * PROJ_TN)
        oh_ref[:, cs] = jnp.dot(hb, w_ref[:, ws], preferred_element_type=F32).astype(BF16)


def _in_proj(s, mod, ng, w, *, tm, n_tiles, tiles_per_sample, n_samples):
    rows = n_tiles * tm
    return pl.pallas_call(
        _inproj_kernel,
        grid=(n_tiles,),
        in_specs=[
            pl.BlockSpec((tm, D_MODEL), lambda i: (i, 0)),
            pl.BlockSpec((1, N_MOD, D_MODEL),
                         lambda i: (jnp.minimum(i // tiles_per_sample, n_samples), 0, 0)),
            _const_spec(ng.shape),
            _const_spec(w.shape),
        ],
        out_specs=[
            pl.BlockSpec((tm, F_W), lambda i: (i, 0)),
            pl.BlockSpec((tm, H_W), lambda i: (i, 0)),
        ],
        out_shape=[jax.ShapeDtypeStruct((rows, F_W), F32),
                   jax.ShapeDtypeStruct((rows, H_W), BF16)],
        compiler_params=_cparams(("parallel",)),
        name="in_proj",
    )(s, mod, ng, w)


def _prep_kernel(x_ref, prev_ref, next_ref, small_ref, cw_ref, cb_ref, alog_ref, dtb_ref,
                 cvg_ref, cvs_ref, gb_ref, dt_ref, gt_ref, at_ref, ext_scr, *, nx_tiles,
                 x_tiles_per_seq, c_tiles_per_seq):
    i = pl.program_id(0)
    t = x_ref.shape[0]
    pos = jnp.where(i < nx_tiles, i % x_tiles_per_seq, (i - nx_tiles) % c_tiles_per_seq)
    last = jnp.where(i < nx_tiles, x_tiles_per_seq - 1, c_tiles_per_seq - 1)
    has_prev = (pos != 0).astype(F32)
    has_next = (pos != last).astype(F32)
    pad = CONV_K // 2

    ncol = CONV_W // 512
    for cc in range(ncol):
        cs = slice(cc * 512, (cc + 1) * 512)
        ext_scr[0:8, :] = prev_ref[:, cs] * has_prev
        ext_scr[8:8 + t, :] = x_ref[:, cs]
        ext_scr[8 + t:16 + t, :] = next_ref[:, cs] * has_next
        y = jnp.broadcast_to(cb_ref[:, cs], (t, 512))
        for k in range(CONV_K):
            off = 8 - pad + k
            y = y + cw_ref[k:k + 1, cs] * ext_scr[off:off + t, :]
        y = _silu(y)
        if cc < 2:
            for h in range(GDN_HEADS):
                hs = slice(h * GDN_DK, (h + 1) * GDN_DK)
                blk = y[:, hs]
                n = blk * lax.rsqrt(jnp.sum(blk * blk, axis=-1, keepdims=True) + EPS)
                if cc == 0:
                    n = n * (GDN_DK ** -0.5)
                cvg_ref[:, cc * 512 + h * GDN_DK:cc * 512 + (h + 1) * GDN_DK] = n
        elif cc == 2:
            cvg_ref[:, cs] = y
        else:
            cvs_ref[:, (cc - 3) * 512:(cc - 2) * 512] = y

    sm = small_ref[...]
    lane = lax.broadcasted_iota(jnp.int32, (t, 128), 1)
    row = lax.broadcasted_iota(jnp.int32, (t, 128), 0)
    z = sm + dtb_ref[...]
    sp = jnp.maximum(z, 0.0) + jnp.log1p(jnp.exp(-jnp.abs(z)))
    g = -jnp.exp(alog_ref[...]) * sp
    is_gdn = lane < LANE_GB
    is_ssd = (lane >= LANE_DT) & (lane < LANE_DT + 2 * SSD_HEADS)
    win = jnp.where(is_gdn, GDN_CHUNK, SSD_CHUNK)
    rmod = row % win
    g = jnp.where(is_gdn | is_ssd, g, 0.0)
    pre, suf = g, g
    sft = 1
    while sft < SSD_CHUNK:
        pre = pre + jnp.where(rmod >= sft, pltpu.roll(pre, sft, 0), 0.0)
        suf = suf + jnp.where(rmod < win - sft, pltpu.roll(suf, t - sft, 0), 0.0)
        sft *= 2
    backward = (is_gdn & (lane >= GDN_HEADS)) | (is_ssd & (lane >= LANE_DT + SSD_HEADS))
    cum = jnp.where(backward, suf, pre)
    is_beta = (lane >= LANE_GB) & (lane < LANE_DT)
    gb = jnp.where(is_beta, jax.nn.sigmoid(sm), cum)
    gb_ref[...] = gb
    dt_ref[...] = sp
    gbt = gb.T
    for c in range(t // GDN_CHUNK):
        gt_ref[c] = gbt[0:16, c * GDN_CHUNK:(c + 1) * GDN_CHUNK]
    for c in range(t // SSD_CHUNK):
        at_ref[c] = gbt[16:32, c * SSD_CHUNK:(c + 1) * SSD_CHUNK]


def _prep(f, cw, cb, alog, dtb, *, n_tiles, nx_tiles, x_tiles_per_seq, c_tiles_per_seq):
    t = SEQ_T
    rows = n_tiles * t
    r8 = rows // 8
    t8 = t // 8
    return pl.pallas_call(
        functools.partial(_prep_kernel, nx_tiles=nx_tiles, x_tiles_per_seq=x_tiles_per_seq,
                          c_tiles_per_seq=c_tiles_per_seq),
        grid=(n_tiles,),
        in_specs=[
            pl.BlockSpec((t, CONV_W), lambda i: (i, 0)),
            pl.BlockSpec((8, CONV_W), lambda i: (jnp.maximum(i * t8 - 1, 0), 0)),
            pl.BlockSpec((8, CONV_W), lambda i: (jnp.minimum((i + 1) * t8, r8 - 1), 0)),
            pl.BlockSpec((t, 128), lambda i: (i, F_SMALL // 128)),
            _const_spec(cw.shape),
            _const_spec(cb.shape),
            _const_spec(alog.shape),
            _const_spec(dtb.shape),
        ],
        out_specs=[
            pl.BlockSpec((t, GDN_QKV), lambda i: (i, 0)),
            pl.BlockSpec((t, SSD_XBC), lambda i: (i, 0)),
            pl.BlockSpec((t, 128), lambda i: (i, 0)),
            pl.BlockSpec((t, 128), lambda i: (i, 0)),
            pl.BlockSpec((t // GDN_CHUNK, 16, GDN_CHUNK), lambda i: (i, 0, 0)),
            pl.BlockSpec((t // SSD_CHUNK, 16, SSD_CHUNK), lambda i: (i, 0, 0)),
        ],
        out_shape=[
            jax.ShapeDtypeStruct((rows, GDN_QKV), F32),
            jax.ShapeDtypeStruct((rows, SSD_XBC), F32),
            jax.ShapeDtypeStruct((rows, 128), F32),
            jax.ShapeDtypeStruct((rows, 128), F32),
            jax.ShapeDtypeStruct((rows // GDN_CHUNK, 16, GDN_CHUNK), F32),
            jax.ShapeDtypeStruct((rows // SSD_CHUNK, 16, SSD_CHUNK), F32),
        ],
        scratch_shapes=[pltpu.VMEM((t + 16, 512), F32)],
        compiler_params=_cparams(("parallel",)),
        name="conv_prep",
    )(f, f, f, f, cw, cb, alog, dtb)


def _rope_kernel(x_ref, v_ref, cos_ref, sinp_ref, sinm_ref, q_ref, kv_ref, *, nx_tiles):
    i = pl.program_id(0)
    is_x = i < nx_tiles
    cos, sinp, sinm = cos_ref[...], sinp_ref[...], sinm_ref[...]
    nblk = x_ref.shape[1] // 128
    for b in range(nblk):
        bs = slice(b * 128, (b + 1) * 128)
        v = x_ref[:, bs]
        r = v * cos + pltpu.roll(v, 32, 1) * sinp + pltpu.roll(v, 96, 1) * sinm
        r = jnp.where(is_x, r, v)
        if b < nblk // 2:
            q_ref[:, bs] = (r * Q_SCALE).astype(BF16)
        else:
            kv_ref[:, (b - nblk // 2) * 128:(b - nblk // 2 + 1) * 128] = r.astype(BF16)
    kv_ref[:, 512:] = v_ref[...]


def _rope(f, hh, cos, sinp, sinm, *, n_tiles, nx_tiles, x_tiles_per_seq, c_tiles_per_seq):
    t = SEQ_T
    per_sample = x_tiles_per_seq + c_tiles_per_seq
    tab = pl.BlockSpec((t, 128), lambda i: (jnp.where(i < nx_tiles, i % x_tiles_per_seq, 0), 0))

    def kv_block(i):
        j = i - nx_tiles
        return jnp.where(i < nx_tiles,
                         (i // x_tiles_per_seq) * per_sample + i % x_tiles_per_seq,
                         (j // c_tiles_per_seq) * per_sample + x_tiles_per_seq + j % c_tiles_per_seq)

    return pl.pallas_call(
        functools.partial(_rope_kernel, nx_tiles=nx_tiles),
        grid=(n_tiles,),
        in_specs=[pl.BlockSpec((t, 1024), lambda i: (i, F_DQ // 1024)),
                  pl.BlockSpec((t, 512), lambda i: (i, H_DV // 512)), tab, tab, tab],
        out_specs=[pl.BlockSpec((t, 512), lambda i: (i, 0)),
                   pl.BlockSpec((t, 1024), lambda i: (kv_block(i), 0))],
        out_shape=[jax.ShapeDtypeStruct((n_tiles * t, 512), BF16),
                   jax.ShapeDtypeStruct((n_tiles * t, 1024), BF16)],
        compiler_params=_cparams(("parallel",)),
        name="rope",
    )(f, hh, cos, sinp, sinm)


def _attn_kernel(lam_ref, nw_ref, q_ref, k_ref, v_ref, *rest, tk, n_k, lam_init):
    o_ref, s_scr, acc_scr, m_scr, l_scr = rest[-5:]
    q = q_ref[...]
    lane = lax.broadcasted_iota(jnp.int32, q.shape, 1)
    zero = jnp.zeros_like(q)
    qh = (jnp.where(lane < DIFF_DQK, q, zero), jnp.where(lane >= DIFF_DQK, q, zero))
    acc_scr[...] = jnp.zeros_like(acc_scr)
    l_scr[...] = jnp.zeros_like(l_scr)
    m_scr[...] = jnp.full(m_scr.shape, -jnp.inf, F32)

    def key_rows(t):
        r0 = t * tk
        if not isinstance(t, int):
            r0 = pl.multiple_of(r0, tk)
        return pl.ds(r0, tk)

    def scores(t, slot):
        k = k_ref[key_rows(t), :]
        for j in range(2):
            s_scr[slot, j] = lax.dot_general(qh[j], k, (((1,), (1,)), ((), ())),
                                             preferred_element_type=F32)

    def consume(t, slot):
        v = v_ref[key_rows(t), :]
        for j in range(2):
            s = s_scr[slot, j]
            m_prev = m_scr[j]
            m_new = jnp.maximum(m_prev, jnp.max(s, axis=1, keepdims=True))
            alpha = jnp.exp2(m_prev - m_new)
            p = jnp.exp2(s - jnp.concatenate([m_new] * (tk // 128), axis=1))
            l_scr[j] = alpha * l_scr[j] + jnp.sum(p, axis=1, keepdims=True)
            m_scr[j] = m_new
            acc_scr[j] = alpha * acc_scr[j] + jnp.dot(p.astype(BF16), v,
                                                      preferred_element_type=F32)

    scores(0, 0)
    n_pair = (n_k - 1) // 2

    def body(u, carry):
        t = 2 * u
        scores(t + 1, 1)
        consume(t, 0)
        scores(t + 2, 0)
        consume(t + 1, 1)
        return carry

    if n_pair > 0:
        lax.fori_loop(0, n_pair, body, 0)
    if n_k % 2 == 1:
        consume(n_k - 1, 0)
    else:
        scores(n_k - 1, 1)
        consume(n_k - 2, 0)
        consume(n_k - 1, 1)

    lam = lam_ref[...]
    lam_full = (jnp.exp(jnp.sum(lam[0:1] * lam[1:2], axis=-1, keepdims=True))
                - jnp.exp(jnp.sum(lam[2:3] * lam[3:4], axis=-1, keepdims=True)) + lam_init)
    o = acc_scr[0] / l_scr[0] - lam_full * (acc_scr[1] / l_scr[1])
    o_ref[...] = (_rms(o) * nw_ref[...] * (1.0 - lam_init)).astype(BF16)


def _attn_scratch(tq, tk):
    return [pltpu.VMEM((2, 2, tq, tk), F32), pltpu.VMEM((2, tq, 128), F32),
            pltpu.VMEM((2, tq, 128), F32), pltpu.VMEM((2, tq, 128), F32)]


def _attention(q, kv, lam, nw, *, bsz, seq, ctx_len, lam_init, with_ctx):
    tq = 512
    n_keys = seq + ctx_len
    tk = 768 if n_keys % 768 == 0 else ctx_len
    rows = bsz * n_keys
    nq = seq // tq
    out = pl.pallas_call(
        functools.partial(_attn_kernel, tk=tk, n_k=n_keys // tk, lam_init=lam_init),
        grid=(bsz, DIFF_HEADS, nq),
        in_specs=[
            _const_spec(lam.shape),
            _const_spec(nw.shape),
            pl.BlockSpec((tq, 128), lambda b, h, i: (b * nq + i, h)),
            pl.BlockSpec((n_keys, 128), lambda b, h, i: (b, h)),
            pl.BlockSpec((n_keys, 128), lambda b, h, i: (b, DIFF_HEADS + h)),
        ],
        out_specs=pl.BlockSpec((tq, 128), lambda b, h, i: (b * nq + i, h)),
        out_shape=jax.ShapeDtypeStruct((rows, DIFF_HEADS * DIFF_DV), BF16),
        scratch_shapes=_attn_scratch(tq, tk),
        compiler_params=_cparams(("parallel", "parallel", "arbitrary")),
        name="diff_attn",
    )(lam, nw, q, kv, kv)
    if not with_ctx:
        return out
    qblk = bsz * seq // ctx_len
    per_sample = n_keys // ctx_len

    def ctx_keys(col):
        return lambda b, h: (b * per_sample + per_sample - 1, col + h)

    return pl.pallas_call(
        functools.partial(_attn_kernel, tk=ctx_len, n_k=1, lam_init=lam_init),
        grid=(bsz, DIFF_HEADS),
        in_specs=[
            _const_spec(lam.shape),
            _const_spec(nw.shape),
            pl.BlockSpec((ctx_len, 128), lambda b, h: (qblk + b, h)),
            pl.BlockSpec((ctx_len, 128), ctx_keys(0)),
            pl.BlockSpec((ctx_len, 128), ctx_keys(DIFF_HEADS)),
            pl.BlockSpec(memory_space=pl.ANY),
        ],
        out_specs=pl.BlockSpec((ctx_len, 128), lambda b, h: (qblk + b, h)),
        out_shape=jax.ShapeDtypeStruct((rows, DIFF_HEADS * DIFF_DV), BF16),
        scratch_shapes=_attn_scratch(ctx_len, ctx_len),
        input_output_aliases={5: 0},
        compiler_params=_cparams(("parallel", "parallel")),
        name="diff_attn_ctx",
    )(lam, nw, q, kv, kv, out)


def _seq_maps(bsz, nsx, nsc):
    cbase = bsz * nsx

    def fwd(b, s):
        return jnp.where(s < nsc, cbase + b * nsc + s, b * nsx + (s - nsc))

    def bwd(b, s):
        return jnp.where(s < nsc, cbase + b * nsc + (nsc - 1 - s), b * nsx + (nsx - 1 - (s - nsc)))

    return fwd, bwd


def _gdn_kernel(cvf_ref, gbf_ref, gtf_ref, cvb_ref, gbb_ref, gtb_ref, of_ref, ob_ref, st_ref):
    @pl.when(pl.program_id(1) == 0)
    def _():
        st_ref[...] = jnp.zeros_like(st_ref)

    c_ = GDN_CHUNK
    nck = cvf_ref.shape[0] // c_
    ii = lax.broadcasted_iota(jnp.int32, (c_, c_), 0)
    jj = lax.broadcasted_iota(jnp.int32, (c_, c_), 1)
    eye = (ii == jj).astype(F32)
    blk16 = (ii // 16) == (jj // 16)
    same32 = (ii // 32) == (jj // 32)
    lvl32 = same32 & jnp.logical_not(blk16)
    lvl64 = jnp.logical_not(same32)
    dirs = (
        (cvf_ref, gbf_ref, gtf_ref, of_ref, ii >= jj, ii > jj, c_ - 1),
        (cvb_ref, gbb_ref, gtb_ref, ob_ref, ii <= jj, ii < jj, 0),
    )

    class _P:
        pass

    probs = []
    for d, (cv_ref, gb_ref, gt_ref, o_ref, incl, strict, last_row) in enumerate(dirs):
        for c in range(nck):
            cc = c if d == 0 else nck - 1 - c
            for h in range(GDN_HEADS):
                p = _P()
                p.d, p.h, p.c, p.cc = d, h, c, cc
                p.cv, p.gb, p.gt, p.o = cv_ref, gb_ref, gt_ref, o_ref
                p.incl, p.strict, p.last_row = incl, strict, last_row
                p.rows = slice(cc * c_, (cc + 1) * c_)
                p.idx = d * GDN_HEADS + h
                probs.append(p)

    def q_of(p):
        return p.cv[p.rows, p.h * GDN_DK:(p.h + 1) * GDN_DK]

    def k_of(p):
        return p.cv[p.rows, 512 + p.h * GDN_DK:512 + (p.h + 1) * GDN_DK]

    def v_of(p):
        return p.cv[p.rows, 1024 + p.h * GDN_DV:1024 + (p.h + 1) * GDN_DV]

    def gcol_of(p):
        return p.gb[p.rows, :][:, LANE_GA + p.idx:LANE_GA + p.idx + 1]

    def beta_of(p):
        return p.gb[p.rows, :][:, LANE_GB + p.idx:LANE_GB + p.idx + 1]

    def glast_of(p):
        r = p.last_row
        return p.gb[p.rows, :][r:r + 1, LANE_GA + p.idx:LANE_GA + p.idx + 1]

    for p in probs:
        kh = k_of(p)
        grow = p.gt[p.cc][p.idx:p.idx + 1, :]
        dec = jnp.where(p.incl, jnp.exp(jnp.minimum(gcol_of(p) - grow, 0.0)), 0.0)
        ga = _bdot_nt(jnp.concatenate([kh * beta_of(p), q_of(p)], axis=0), kh)
        p.m = jnp.where(p.strict, ga[:c_] * dec, 0.0)
        p.aqk = ga[c_:] * dec
    for p in probs:
        mb = jnp.where(blk16, p.m, 0.0)
        p.tinv = eye - mb
        p.sq = _bdot(mb, mb)
    for step in range(3):
        for p in probs:
            p.tinv = p.tinv + _bdot(p.tinv, p.sq)
            if step < 2:
                p.sq = _bdot(p.sq, p.sq)
    for lvl in (lvl32, lvl64):
        for p in probs:
            p.x = _bdot(jnp.where(lvl, p.m, 0.0), p.tinv)
        for p in probs:
            p.tinv = p.tinv - _bdot(p.tinv, p.x)
    for p in probs:
        beta = beta_of(p)
        rhs = jnp.concatenate([v_of(p) * beta, k_of(p) * (beta * jnp.exp(gcol_of(p)))], axis=1)
        p.uw = _bdot(p.tinv, rhs)
    states = [[st_ref[d, h] for h in range(GDN_HEADS)] for d in range(2)]
    for c in range(nck):
        cur = [p for p in probs if p.c == c]
        for p in cur:
            lhs = jnp.concatenate([p.uw[:, GDN_DV:], q_of(p) * jnp.exp(gcol_of(p))], axis=0)
            p.wq = _bdot(lhs, states[p.d][p.h])
        for p in cur:
            p.v_new = p.uw[:, :GDN_DV] - p.wq[:c_]
            glast = glast_of(p)
            k_dec = k_of(p) * jnp.exp(glast - gcol_of(p))
            states[p.d][p.h] = (states[p.d][p.h] * jnp.exp(glast)
                                + _bdot_tn(k_dec, p.v_new))
        for p in cur:
            p.out = p.wq[c_:] + _bdot(p.aqk, p.v_new)
    for p in probs:
        p.o[p.rows, p.h * GDN_DV:(p.h + 1) * GDN_DV] = p.out
    for d in range(2):
        for h in range(GDN_HEADS):
            st_ref[d, h] = states[d][h]


def _gdn(cv, gb, gt, *, bsz, nsx, nsc):
    t = SEQ_T
    rows = cv.shape[0]
    fwd, bwd = _seq_maps(bsz, nsx, nsc)
    nck = t // GDN_CHUNK

    def specs(fn):
        return [
            pl.BlockSpec((t, GDN_QKV), lambda b, s: (fn(b, s), 0)),
            pl.BlockSpec((t, 128), lambda b, s: (fn(b, s), 0)),
            pl.BlockSpec((nck, 16, GDN_CHUNK), lambda b, s: (fn(b, s), 0, 0)),
        ]

    return pl.pallas_call(
        _gdn_kernel,
        grid=(bsz, nsc + nsx),
        in_specs=specs(fwd) + specs(bwd),
        out_specs=[pl.BlockSpec((t, 512), lambda b, s: (fwd(b, s), 0)),
                   pl.BlockSpec((t, 512), lambda b, s: (bwd(b, s), 0))],
        out_shape=[jax.ShapeDtypeStruct((rows, 512), F32)] * 2,
        scratch_shapes=[pltpu.VMEM((2, GDN_HEADS, GDN_DK, GDN_DV), F32)],
        compiler_params=_cparams(("parallel", "arbitrary")),
        name="gdn_scan",
    )(cv, gb, gt, cv, gb, gt)


def _ssd_kernel(dsk_ref, cvf_ref, gbf_ref, dtf_ref, atf_ref, cvb_ref, gbb_ref, dtb_ref, atb_ref,
                yf_ref, yb_ref, st_ref):
    @pl.when(pl.program_id(1) == 0)
    def _():
        st_ref[...] = jnp.zeros_like(st_ref)

    c_ = SSD_CHUNK
    p_ = SSD_HEADDIM
    hpg = SSD_HEADS // SSD_GROUPS
    nck = cvf_ref.shape[0] // c_
    ii = lax.broadcasted_iota(jnp.int32, (c_, c_), 0)
    jj = lax.broadcasted_iota(jnp.int32, (c_, c_), 1)
    dirs = (
        (cvf_ref, gbf_ref, dtf_ref, atf_ref, yf_ref, ii >= jj, c_ - 1),
        (cvb_ref, gbb_ref, dtb_ref, atb_ref, yb_ref, ii <= jj, 0),
    )

    states = [[st_ref[d, g] for g in range(SSD_GROUPS)] for d in range(2)]
    stores = []
    for d, (cv_ref, gb_ref, dt_ref, at_ref, y_ref, incl, last_row) in enumerate(dirs):
        for c in range(nck):
            cc = c if d == 0 else nck - 1 - c
            rows = slice(cc * c_, (cc + 1) * c_)
            acs = gb_ref[rows, :]
            dts = dt_ref[rows, :]
            at = at_ref[cc]
            for g in range(SSD_GROUPS):
                bm = cv_ref[rows, SSD_INNER + g * SSD_STATE:SSD_INNER + (g + 1) * SSD_STATE]
                cm = cv_ref[rows, SSD_INNER + (SSD_GROUPS + g) * SSD_STATE:
                            SSD_INNER + (SSD_GROUPS + g + 1) * SSD_STATE]
                cb = _bdot_nt(cm, bm)
                s_old = states[d][g]
                y_off = _bdot(cm, s_old)
                xd, sdec, ys = [], [], []
                for hh in range(hpg):
                    h = g * hpg + hh
                    idx = d * SSD_HEADS + h
                    ln = LANE_DT + idx
                    acol = acs[:, ln:ln + 1]
                    arow = at[idx:idx + 1, :]
                    alast = acs[last_row:last_row + 1, ln:ln + 1]
                    xh = cv_ref[rows, h * p_:(h + 1) * p_]
                    xdt = xh * dts[:, ln:ln + 1]
                    ldec = jnp.where(incl, jnp.exp(jnp.minimum(acol - arow, 0.0)), 0.0)
                    y = _bdot(cb * ldec, xdt) + y_off[:, hh * p_:(hh + 1) * p_] * jnp.exp(acol)
                    if d == 0:
                        y = y + dsk_ref[:, h * p_:(h + 1) * p_] * xh
                    ys.append(y)
                    xd.append(xdt * jnp.exp(alast - acol))
                    sdec.append(jnp.broadcast_to(jnp.exp(alast), (1, p_)))
                stores.append((y_ref, rows, slice(g * hpg * p_, (g + 1) * hpg * p_),
                               jnp.concatenate(ys, axis=1)))
                states[d][g] = (s_old * jnp.concatenate(sdec, axis=1)
                                + _bdot_tn(bm, jnp.concatenate(xd, axis=1)))
    for y_ref, rows, cols, val in stores:
        y_ref[rows, cols] = val
    for d in range(2):
        for g in range(SSD_GROUPS):
            st_ref[d, g] = states[d][g]


def _ssd(dsk, cv, gb, dt, at, *, bsz, nsx, nsc):
    t = SEQ_T
    rows = cv.shape[0]
    fwd, bwd = _seq_maps(bsz, nsx, nsc)
    nck = t // SSD_CHUNK

    def specs(fn):
        return [
            pl.BlockSpec((t, SSD_XBC), lambda b, s: (fn(b, s), 0)),
            pl.BlockSpec((t, 128), lambda b, s: (fn(b, s), 0)),
            pl.BlockSpec((t, 128), lambda b, s: (fn(b, s), 0)),
            pl.BlockSpec((nck, 16, SSD_CHUNK), lambda b, s: (fn(b, s), 0, 0)),
        ]

    return pl.pallas_call(
        _ssd_kernel,
        grid=(bsz, nsc + nsx),
        in_specs=[_const_spec(dsk.shape)] + specs(fwd) + specs(bwd),
        out_specs=[pl.BlockSpec((t, SSD_INNER), lambda b, s: (fwd(b, s), 0)),
                   pl.BlockSpec((t, SSD_INNER), lambda b, s: (bwd(b, s), 0))],
        out_shape=[jax.ShapeDtypeStruct((rows, SSD_INNER), F32)] * 2,
        scratch_shapes=[pltpu.VMEM((2, SSD_GROUPS, SSD_STATE, SSD_INNER // SSD_GROUPS), F32)],
        compiler_params=_cparams(("parallel", "arbitrary")),
        name="ssd_scan",
    )(dsk, cv, gb, dt, at, cv, gb, dt, at)


def _merge_kernel(x_ref, mod_ref, ng_ref, gof_ref, gob_ref, gz_ref, att_ref, syf_ref, syb_ref,
                  sz_ref, gate_ref, gnw_ref, snw_ref, wb_ref, wo_ref, o_ref):
    oa = gof_ref[...] + gob_ref[...]
    parts = []
    for h in range(GDN_HEADS):
        hs = slice(h * GDN_DV, (h + 1) * GDN_DV)
        parts.append(_rms(oa[:, hs]) * gnw_ref[:, hs])
    oa = jnp.concatenate(parts, axis=1) * _silu(gz_ref[...].astype(F32))
    oc = (syf_ref[...] + syb_ref[...]) * _silu(sz_ref[...].astype(F32))
    oc = _rms(oc) * snw_ref[...]
    outs = (oa.astype(BF16), att_ref[...], oc.astype(BF16))
    y = None
    for n in range(N_BRANCH):
        gate = jax.nn.sigmoid(gate_ref[:, n * D_MODEL:(n + 1) * D_MODEL].astype(F32))
        term = gate * jnp.dot(outs[n], wb_ref[n], preferred_element_type=F32)
        y = term if y is None else y + term
    y = jnp.dot(y.astype(BF16), wo_ref[...], preferred_element_type=F32)
    m = mod_ref[0]
    o_ref[...] = x_ref[...] + m[5:6] * (_rms(y) * ng_ref[3:4, :])


def _merge(s, mod, ng, gof, gob, hh, att, syf, syb, gnw, snw, wb, wo, *, tm, n_tiles,
           tiles_per_sample, n_samples):
    rows = n_tiles * tm
    row512 = pl.BlockSpec((tm, 512), lambda i: (i, 0))
    return pl.pallas_call(
        _merge_kernel,
        grid=(n_tiles,),
        in_specs=[
            pl.BlockSpec((tm, D_MODEL), lambda i: (i, 0)),
            pl.BlockSpec((1, N_MOD, D_MODEL),
                         lambda i: (jnp.minimum(i // tiles_per_sample, n_samples), 0, 0)),
            _const_spec(ng.shape),
            row512, row512,
            pl.BlockSpec((tm, 512), lambda i: (i, H_GZ // 512)),
            row512, row512, row512,
            pl.BlockSpec((tm, 512), lambda i: (i, H_SZ // 512)),
            pl.BlockSpec((tm, N_BRANCH * D_MODEL), lambda i: (i, 0)),
            _const_spec(gnw.shape),
            _const_spec(snw.shape),
            _const_spec(wb.shape),
            _const_spec(wo.shape),
        ],
        out_specs=pl.BlockSpec((tm, D_MODEL), lambda i: (i, 0)),
        out_shape=jax.ShapeDtypeStruct((rows, D_MODEL), F32),
        compiler_params=_cparams(("parallel",)),
        name="merge",
    )(s, mod, ng, gof, gob, hh, att, syf, syb, hh, hh, gnw, snw, wb, wo)


def _deinterleave_pairs(w):
    d = w.shape[0]
    w = w.reshape(d, DIFF_HEADS, 2, DIFF_DQK // 2, 2)
    return jnp.swapaxes(w, 3, 4).reshape(d, DIFF_HEADS * 2 * DIFF_DQK)


def _proj_weight(w_in):
    o = 0
    parts = {}
    for name, width in (("gqkv", GDN_QKV), ("gz", 512), ("ga", 8), ("gb", 8), ("dq", 512),
                        ("dk", 512), ("dv", 512), ("sz", 512), ("sxbc", SSD_XBC), ("sdt", 16),
                        ("gate", N_BRANCH * D_MODEL)):
        parts[name] = w_in[:, o:o + width]
        o += width
    small_pad = jnp.zeros((w_in.shape[0], SMALL_W - 32), w_in.dtype)
    cols = [parts["gqkv"], parts["sxbc"], parts["ga"], parts["gb"], parts["sdt"], small_pad,
            _deinterleave_pairs(parts["dq"]), _deinterleave_pairs(parts["dk"]),
            parts["gate"], parts["dv"], parts["gz"], parts["sz"]]
    return jnp.concatenate(cols, axis=1).astype(BF16)


def _rope_tables(seq):
    rows = seq // GRID_W
    row = jnp.repeat(jnp.arange(rows, dtype=F32), GRID_W)
    col = jnp.tile(jnp.arange(GRID_W, dtype=F32), rows)
    inv = ROPE_BASE ** (-jnp.arange(ROPE_PAIRS_PER_AXIS, dtype=F32) / ROPE_PAIRS_PER_AXIS)
    ang = jnp.concatenate([row[:, None] * inv, col[:, None] * inv], axis=-1)
    cos, sin = jnp.cos(ang), jnp.sin(ang)
    zero = jnp.zeros_like(sin)
    cos = jnp.tile(cos, (1, 4))
    sinp = jnp.tile(jnp.concatenate([zero, sin], axis=1), (1, 2))
    sinm = jnp.tile(jnp.concatenate([-sin, zero], axis=1), (1, 2))
    return cos, sinp, sinm


def _lane_vec(pieces):
    v = jnp.zeros((128,), F32)
    for off, val in pieces:
        v = lax.dynamic_update_slice(v, val.reshape(-1).astype(F32), (off,))
    return v.reshape(1, 128)


def kernel(x, c, ctx, c_ctx, w_ada, b_ada, norm_g, w_ffn_in, w_ffn_out, w_in, gdn_conv_w, gdn_a_log,
           gdn_dt_bias, gdn_norm_w, diff_lambda, diff_norm_w, ssd_conv_w, ssd_conv_b, ssd_a_log,
           ssd_dt_bias, ssd_d, ssd_norm_w, w_branch, w_out):
    bsz, seq, _ = x.shape
    ctx_len = ctx.shape[1]
    t = SEQ_T
    tm = 512
    assert ctx_len == t and seq % 512 == 0 and (bsz * ctx_len) % tm == 0 and seq % GRID_W == 0
    nx_rows, nc_rows = bsz * seq, bsz * ctx_len
    n_rows = nx_rows + nc_rows

    s = jnp.concatenate([x.reshape(nx_rows, D_MODEL), ctx.reshape(nc_rows, D_MODEL)], axis=0)
    nm = -(-(bsz + 1) // 8) * 8
    cvec = jnp.zeros((nm, D_MODEL), F32).at[:bsz].set(c).at[bsz].set(c_ctx)
    mod_all = _mod_table(cvec, w_ada, b_ada)
    cos, sinp, sinm = _rope_tables(seq)

    tok = dict(tm=tm, tiles_per_sample=seq // tm, n_samples=bsz)
    all_tiles, x_tiles = n_rows // tm, nx_rows // tm
    seq_kw = dict(n_tiles=n_rows // t, nx_tiles=nx_rows // t, x_tiles_per_seq=seq // t)
    nsx, nsc = seq // t, ctx_len // t

    for i in range(DEPTH):
        last = i == DEPTH - 1
        lam_init = 0.8 - 0.6 * math.exp(-0.3 * i)
        mod, ng = mod_all[i], norm_g[i]
        w1 = [w_ffn_in[i, j].astype(BF16) for j in range(2)]
        w2 = [w_ffn_out[i, j].astype(BF16) for j in range(2)]

        s = _half_ffn(s, mod, ng, w1[0], w2[0], slot=0, gslot=0, n_tiles=all_tiles, **tok)

        f, hh = _in_proj(s, mod, ng, _proj_weight(w_in[i]), n_tiles=all_tiles, **tok)
        cw = jnp.concatenate([gdn_conv_w[i], ssd_conv_w[i]], axis=1)
        cb = jnp.concatenate([jnp.zeros((GDN_QKV,), F32), ssd_conv_b[i]]).reshape(1, CONV_W)
        alog = _lane_vec([(LANE_GA, gdn_a_log[i]), (LANE_DT, ssd_a_log[i])])
        dtb = _lane_vec([(LANE_GA, gdn_dt_bias[i]), (LANE_DT, ssd_dt_bias[i])])
        cvg, cvs, gb, dt, gt, at = _prep(f, cw, cb, alog, dtb, c_tiles_per_seq=ctx_len // t, **seq_kw)
        qr, kv = _rope(f, hh, cos, sinp, sinm, c_tiles_per_seq=ctx_len // t, **seq_kw)

        gof, gob = _gdn(cvg, gb, gt, bsz=bsz, nsx=nsx, nsc=nsc)
        dsk = jnp.repeat(ssd_d[i].astype(F32), SSD_HEADDIM).reshape(1, SSD_INNER)
        syf, syb = _ssd(dsk, cvs, gb, dt, at, bsz=bsz, nsx=nsx, nsc=nsc)
        att = _attention(qr, kv, diff_lambda[i], diff_norm_w[i].reshape(1, DIFF_DV), bsz=bsz,
                         seq=seq, ctx_len=ctx_len, lam_init=lam_init, with_ctx=not last)

        n_out = x_tiles if last else all_tiles
        gnw = jnp.tile(gdn_norm_w[i], GDN_HEADS).reshape(1, GDN_HEADS * GDN_DV)
        snw = ssd_norm_w[i].reshape(1, SSD_INNER)
        s = _merge(s, mod, ng, gof, gob, hh, att, syf, syb, gnw, snw, w_branch[i].astype(BF16),
                   w_out[i].astype(BF16), n_tiles=n_out, **tok)
        s = _half_ffn(s, mod, ng, w1[1], w2[1], slot=6, gslot=4, n_tiles=n_out, **tok)

    return s[:nx_rows].reshape(bsz, seq, D_MODEL)
```

```python
import functools
import math

import jax
import jax.numpy as jnp
from jax import lax
from jax.experimental import pallas as pl
from jax.experimental.pallas import tpu as pltpu

F32 = jnp.float32
BF16 = jnp.bfloat16

D_MODEL = 1024
DEPTH = 2
GRID_W = 64
EPS = 1e-6
N_MOD = 9
D_FF = 2816
CONV_K = 5
GDN_HEADS = 4
GDN_DK = 128
GDN_DV = 128
GDN_CHUNK = 64
DIFF_HEADS = 4
DIFF_DQK = 64
DIFF_DV = 128
ROPE_BASE = 10000.0
ROPE_PAIRS_PER_AXIS = DIFF_DQK // 4
SSD_HEADS = 8
SSD_HEADDIM = 64
SSD_INNER = SSD_HEADS * SSD_HEADDIM
SSD_GROUPS = 2
SSD_STATE = 128
SSD_CHUNK = 128
N_BRANCH = 3
BRANCH_W = 512

GDN_QKV = 2 * GDN_HEADS * GDN_DK + GDN_HEADS * GDN_DV
SSD_XBC = SSD_INNER + 2 * SSD_GROUPS * SSD_STATE
CONV_W = GDN_QKV + SSD_XBC
SMALL_W = 512
F_SMALL = CONV_W
F_DQ = F_SMALL + SMALL_W
F_W = F_DQ + 2 * 512
H_DV = N_BRANCH * D_MODEL
H_GZ = H_DV + 512
H_SZ = H_GZ + 512
H_W = H_SZ + 512
LANE_GA = 0
LANE_GB = 8
LANE_DT = 16

Q_SCALE = DIFF_DQK ** -0.5 * math.log2(math.e)

FF_CHUNK = 256
PROJ_TN = 512
SEQ_T = 256
VMEM_LIMIT = 56 * 1024 * 1024


def _cparams(sem):
    return pltpu.CompilerParams(dimension_semantics=sem, vmem_limit_bytes=VMEM_LIMIT)


def _const_spec(shape):
    nd = len(shape)
    return pl.BlockSpec(shape, lambda *_: (0,) * nd, pipeline_mode=pl.Buffered(1))


def _silu(v):
    return v * jax.nn.sigmoid(v)


def _bdot(a, b):
    return jnp.dot(a.astype(BF16), b.astype(BF16), preferred_element_type=F32)


def _bdot_nt(a, b):
    return lax.dot_general(a.astype(BF16), b.astype(BF16), (((1,), (1,)), ((), ())),
                           preferred_element_type=F32)


def _bdot_tn(a, b):
    return lax.dot_general(a.astype(BF16), b.astype(BF16), (((0,), (0,)), ((), ())),
                           preferred_element_type=F32)


def _rms(v):
    return v * lax.rsqrt(jnp.mean(v * v, axis=-1, keepdims=True) + EPS)


def _mod_kernel(c_ref, w_ref, b_ref, o_ref):
    sc = _silu(c_ref[...])
    o_ref[0, 0] = jnp.dot(sc, w_ref[0], precision=lax.Precision.HIGHEST,
                          preferred_element_type=F32) + b_ref[0]


def _mod_table(cvec, w_ada, b_ada):
    nm = cvec.shape[0]
    out = pl.pallas_call(
        _mod_kernel,
        grid=(DEPTH, N_MOD),
        in_specs=[
            pl.BlockSpec((nm, D_MODEL), lambda l, j: (0, 0)),
            pl.BlockSpec((1, D_MODEL, D_MODEL), lambda l, j: (l, 0, j)),
            pl.BlockSpec((1, 1, D_MODEL), lambda l, j: (l, 0, j)),
        ],
        out_specs=pl.BlockSpec((1, 1, nm, D_MODEL), lambda l, j: (l, j, 0, 0)),
        out_shape=jax.ShapeDtypeStruct((DEPTH, N_MOD, nm, D_MODEL), F32),
        compiler_params=_cparams(("parallel", "parallel")),
        name="mod_table",
    )(cvec, w_ada, b_ada.reshape(DEPTH, 1, N_MOD * D_MODEL))
    return jnp.transpose(out, (0, 2, 1, 3))


def _ffn_kernel(x_ref, mod_ref, ng_ref, w1_ref, w2_ref, o_ref, *, slot, gslot):
    x = x_ref[...]
    m = mod_ref[0]
    shift, scale, gate = m[slot:slot + 1], m[slot + 1:slot + 2], m[slot + 2:slot + 3]
    h = _rms(x) * ng_ref[gslot:gslot + 1, :]
    hb = (h * (1.0 + scale) + shift).astype(BF16)
    acc = None
    for c in range(D_FF // FF_CHUNK):
        lo = c * FF_CHUNK
        g = jnp.dot(hb, w1_ref[:, lo:lo + FF_CHUNK], preferred_element_type=F32)
        u = jnp.dot(hb, w1_ref[:, D_FF + lo:D_FF + lo + FF_CHUNK], preferred_element_type=F32)
        a = (_silu(g) * u).astype(BF16)
        part = jnp.dot(a, w2_ref[lo:lo + FF_CHUNK, :], preferred_element_type=F32)
        acc = part if acc is None else acc + part
    y = _rms(acc) * ng_ref[gslot + 1:gslot + 2, :]
    o_ref[...] = x + 0.5 * gate * y


def _half_ffn(s, mod, ng, w1, w2, *, slot, gslot, tm, n_tiles, tiles_per_sample, n_samples):
    rows = n_tiles * tm
    return pl.pallas_call(
        functools.partial(_ffn_kernel, slot=slot, gslot=gslot),
        grid=(n_tiles,),
        in_specs=[
            pl.BlockSpec((tm, D_MODEL), lambda i: (i, 0)),
            pl.BlockSpec((1, N_MOD, D_MODEL),
                         lambda i: (jnp.minimum(i // tiles_per_sample, n_samples), 0, 0)),
            _const_spec(ng.shape),
            _const_spec(w1.shape),
            _const_spec(w2.shape),
        ],
        out_specs=pl.BlockSpec((tm, D_MODEL), lambda i: (i, 0)),
        out_shape=jax.ShapeDtypeStruct((rows, D_MODEL), F32),
        compiler_params=_cparams(("parallel",)),
        name="half_ffn",
    )(s, mod, ng, w1, w2)


def _inproj_kernel(x_ref, mod_ref, ng_ref, w_ref, of_ref, oh_ref):
    m = mod_ref[0]
    h = _rms(x_ref[...]) * ng_ref[2:3, :]
    hb = (h * (1.0 + m[4:5]) + m[3:4]).astype(BF16)
    for j in range(F_W // PROJ_TN):
        cs = slice(j * PROJ_TN, (j + 1) * PROJ_TN)
        of_ref[:, cs] = jnp.dot(hb, w_ref[:, cs], preferred_element_type=F32)
    for j in range(H_W // PROJ_TN):
        cs = slice(j * PROJ_TN, (j + 1) * PROJ_TN)
        ws = slice(F_W + j * PROJ_TN, F_W + (j + 1) * PROJ_TN)
        oh_ref[:, cs] = jnp.dot(hb, w_ref[:, ws], preferred_element_type=F32).astype(BF16)


def _in_proj(s, mod, ng, w, *, tm, n_tiles, tiles_per_sample, n_samples):
    rows = n_tiles * tm
    return pl.pallas_call(
        _inproj_kernel,
        grid=(n_tiles,),
        in_specs=[
            pl.BlockSpec((tm, D_MODEL), lambda i: (i, 0)),
            pl.BlockSpec((1, N_MOD, D_MODEL),
                         lambda i: (jnp.minimum(i // tiles_per_sample, n_samples), 0, 0)),
            _const_spec(ng.shape),
            _const_spec(w.shape),
        ],
        out_specs=[
            pl.BlockSpec((tm, F_W), lambda i: (i, 0)),
            pl.BlockSpec((tm, H_W), lambda i: (i, 0)),
        ],
        out_shape=[jax.ShapeDtypeStruct((rows, F_W), F32),
                   jax.ShapeDtypeStruct((rows, H_W), BF16)],
        compiler_params=_cparams(("parallel",)),
        name="in_proj",
    )(s, mod, ng, w)


def _prep_kernel(x_ref, prev_ref, next_ref, small_ref, cw_ref, cb_ref, alog_ref, dtb_ref,
                 cvg_ref, cvs_ref, gb_ref, dt_ref, gt_ref, at_ref, ext_scr, *, nx_tiles,
                 x_tiles_per_seq, c_tiles_per_seq):
    i = pl.program_id(0)
    t = x_ref.shape[0]
    pos = jnp.where(i < nx_tiles, i % x_tiles_per_seq, (i - nx_tiles) % c_tiles_per_seq)
    last = jnp.where(i < nx_tiles, x_tiles_per_seq - 1, c_tiles_per_seq - 1)
    has_prev = (pos != 0).astype(F32)
    has_next = (pos != last).astype(F32)
    pad = CONV_K // 2

    ncol = CONV_W // 512
    for cc in range(ncol):
        cs = slice(cc * 512, (cc + 1) * 512)
        ext_scr[0:8, :] = prev_ref[:, cs] * has_prev
        ext_scr[8:8 + t, :] = x_ref[:, cs]
        ext_scr[8 + t:16 + t, :] = next_ref[:, cs] * has_next
        y = jnp.broadcast_to(cb_ref[:, cs], (t, 512))
        for k in range(CONV_K):
            off = 8 - pad + k
            y = y + cw_ref[k:k + 1, cs] * ext_scr[off:off + t, :]
        y = _silu(y)
        if cc < 2:
            for h in range(GDN_HEADS):
                hs = slice(h * GDN_DK, (h + 1) * GDN_DK)
                blk = y[:, hs]
                n = blk * lax.rsqrt(jnp.sum(blk * blk, axis=-1, keepdims=True) + EPS)
                if cc == 0:
                    n = n * (GDN_DK ** -0.5)
                cvg_ref[:, cc * 512 + h * GDN_DK:cc * 512 + (h + 1) * GDN_DK] = n
        elif cc == 2:
            cvg_ref[:, cs] = y
        else:
            cvs_ref[:, (cc - 3) * 512:(cc - 2) * 512] = y

    sm = small_ref[...]
    lane = lax.broadcasted_iota(jnp.int32, (t, 128), 1)
    row = lax.broadcasted_iota(jnp.int32, (t, 128), 0)
    z = sm + dtb_ref[...]
    sp = jnp.maximum(z, 0.0) + jnp.log1p(jnp.exp(-jnp.abs(z)))
    g = -jnp.exp(alog_ref[...]) * sp
    is_gdn = lane < LANE_GB
    is_ssd = (lane >= LANE_DT) & (lane < LANE_DT + 2 * SSD_HEADS)
    win = jnp.where(is_gdn, GDN_CHUNK, SSD_CHUNK)
    rmod = row % win
    g = jnp.where(is_gdn | is_ssd, g, 0.0)
    pre, suf = g, g
    sft = 1
    while sft < SSD_CHUNK:
        pre = pre + jnp.where(rmod >= sft, pltpu.roll(pre, sft, 0), 0.0)
        suf = suf + jnp.where(rmod < win - sft, pltpu.roll(suf, t - sft, 0), 0.0)
        sft *= 2
    backward = (is_gdn & (lane >= GDN_HEADS)) | (is_ssd & (lane >= LANE_DT + SSD_HEADS))
    cum = jnp.where(backward, suf, pre)
    is_beta = (lane >= LANE_GB) & (lane < LANE_DT)
    gb = jnp.where(is_beta, jax.nn.sigmoid(sm), cum)
    gb_ref[...] = gb
    dt_ref[...] = sp
    gbt = gb.T
    for c in range(t // GDN_CHUNK):
        gt_ref[c] = gbt[0:16, c * GDN_CHUNK:(c + 1) * GDN_CHUNK]
    for c in range(t // SSD_CHUNK):
        at_ref[c] = gbt[16:32, c * SSD_CHUNK:(c + 1) * SSD_CHUNK]


def _prep(f, cw, cb, alog, dtb, *, n_tiles, nx_tiles, x_tiles_per_seq, c_tiles_per_seq):
    t = SEQ_T
    rows = n_tiles * t
    r8 = rows // 8
    t8 = t // 8
    return pl.pallas_call(
        functools.partial(_prep_kernel, nx_tiles=nx_tiles, x_tiles_per_seq=x_tiles_per_seq,
                          c_tiles_per_seq=c_tiles_per_seq),
        grid=(n_tiles,),
        in_specs=[
            pl.BlockSpec((t, CONV_W), lambda i: (i, 0)),
            pl.BlockSpec((8, CONV_W), lambda i: (jnp.maximum(i * t8 - 1, 0), 0)),
            pl.BlockSpec((8, CONV_W), lambda i: (jnp.minimum((i + 1) * t8, r8 - 1), 0)),
            pl.BlockSpec((t, 128), lambda i: (i, F_SMALL // 128)),
            _const_spec(cw.shape),
            _const_spec(cb.shape),
            _const_spec(alog.shape),
            _const_spec(dtb.shape),
        ],
        out_specs=[
            pl.BlockSpec((t, GDN_QKV), lambda i: (i, 0)),
            pl.BlockSpec((t, SSD_XBC), lambda i: (i, 0)),
            pl.BlockSpec((t, 128), lambda i: (i, 0)),
            pl.BlockSpec((t, 128), lambda i: (i, 0)),
            pl.BlockSpec((t // GDN_CHUNK, 16, GDN_CHUNK), lambda i: (i, 0, 0)),
            pl.BlockSpec((t // SSD_CHUNK, 16, SSD_CHUNK), lambda i: (i, 0, 0)),
        ],
        out_shape=[
            jax.ShapeDtypeStruct((rows, GDN_QKV), F32),
            jax.ShapeDtypeStruct((rows, SSD_XBC), F32),
            jax.ShapeDtypeStruct((rows, 128), F32),
            jax.ShapeDtypeStruct((rows, 128), F32),
            jax.ShapeDtypeStruct((rows // GDN_CHUNK, 16, GDN_CHUNK), F32),
            jax.ShapeDtypeStruct((rows // SSD_CHUNK, 16, SSD_CHUNK), F32),
        ],
        scratch_shapes=[pltpu.VMEM((t + 16, 512), F32)],
        compiler_params=_cparams(("parallel",)),
        name="conv_prep",
    )(f, f, f, f, cw, cb, alog, dtb)


def _rope_kernel(x_ref, v_ref, cos_ref, sinp_ref, sinm_ref, q_ref, kv_ref, *, nx_tiles):
    i = pl.program_id(0)
    is_x = i < nx_tiles
    cos, sinp, sinm = cos_ref[...], sinp_ref[...], sinm_ref[...]
    nblk = x_ref.shape[1] // 128
    for b in range(nblk):
        bs = slice(b * 128, (b + 1) * 128)
        v = x_ref[:, bs]
        r = v * cos + pltpu.roll(v, 32, 1) * sinp + pltpu.roll(v, 96, 1) * sinm
        r = jnp.where(is_x, r, v)
        if b < nblk // 2:
            q_ref[:, bs] = (r * Q_SCALE).astype(BF16)
        else:
            kv_ref[:, (b - nblk // 2) * 128:(b - nblk // 2 + 1) * 128] = r.astype(BF16)
    kv_ref[:, 512:] = v_ref[...]


def _rope(f, hh, cos, sinp, sinm, *, n_tiles, nx_tiles, x_tiles_per_seq, c_tiles_per_seq):
    t = SEQ_T
    per_sample = x_tiles_per_seq + c_tiles_per_seq
    tab = pl.BlockSpec((t, 128), lambda i: (jnp.where(i < nx_tiles, i % x_tiles_per_seq, 0), 0))

    def kv_block(i):
        j = i - nx_tiles
        return jnp.where(i < nx_tiles,
                         (i // x_tiles_per_seq) * per_sample + i % x_tiles_per_seq,
                         (j // c_tiles_per_seq) * per_sample + x_tiles_per_seq + j % c_tiles_per_seq)

    return pl.pallas_call(
        functools.partial(_rope_kernel, nx_tiles=nx_tiles),
        grid=(n_tiles,),
        in_specs=[pl.BlockSpec((t, 1024), lambda i: (i, F_DQ // 1024)),
                  pl.BlockSpec((t, 512), lambda i: (i, H_DV // 512)), tab, tab, tab],
        out_specs=[pl.BlockSpec((t, 512), lambda i: (i, 0)),
                   pl.BlockSpec((t, 1024), lambda i: (kv_block(i), 0))],
        out_shape=[jax.ShapeDtypeStruct((n_tiles * t, 512), BF16),
                   jax.ShapeDtypeStruct((n_tiles * t, 1024), BF16)],
        compiler_params=_cparams(("parallel",)),
        name="rope",
    )(f, hh, cos, sinp, sinm)


def _when(cond, fn):
    if isinstance(cond, bool):
        if cond:
            fn()
    else:
        pl.when(cond)(fn)


def _attn_kernel(lam_ref, nw_ref, q_ref, k_ref, v_ref, *rest, tq, tk, n_q, n_k, lam_init):
    o_ref, s0, s1, t0, t1, acc_scr, m_scr = rest[-7:]
    s_scr, mt_scr = (s0, s1), (t0, t1)
    lane = lax.broadcasted_iota(jnp.int32, (tq, 128), 1)
    ones_col = (lax.broadcasted_iota(jnp.int32, (tk, 128), 1) == 0).astype(BF16)
    n_steps = n_q * n_k

    def rows_of(i, size):
        r0 = i * size
        if not isinstance(i, int):
            r0 = pl.multiple_of(r0, size)
        return pl.ds(r0, size)

    def split(n):
        return (n // n_k, n % n_k)

    def scores(n, slot):
        qi, t = split(n)
        q = q_ref[rows_of(qi, tq), :]
        zero = jnp.zeros_like(q)
        k = k_ref[rows_of(t, tk), :]
        for j, keep in enumerate((lane < DIFF_DQK, lane >= DIFF_DQK)):
            s = lax.dot_general(jnp.where(keep, q, zero), k, (((1,), (1,)), ((), ())),
                                preferred_element_type=F32)
            s_scr[slot][j] = s
            mt_scr[slot][j] = jnp.broadcast_to(jnp.max(s, axis=1, keepdims=True), (tq, 128))

    def consume(n, slot):
        _, t = split(n)
        v = jnp.concatenate([v_ref[rows_of(t, tk), :], ones_col], axis=1)
        m_prevs = [m_scr[0], m_scr[1]]
        accs = [acc_scr[0], acc_scr[1]]
        for j in range(2):
            m_new = jnp.maximum(m_prevs[j], mt_scr[slot][j])
            m_scr[j] = m_new
            alpha = jnp.exp2(m_prevs[j] - m_new)
            p = jnp.exp2((s_scr[slot][j]
                          - jnp.concatenate([m_new] * (tk // 128), axis=1)).astype(BF16))
            acc_scr[j] = (jnp.concatenate([alpha, alpha], axis=1) * accs[j]
                          + jnp.dot(p, v, preferred_element_type=F32))

    def finalize(qi):
        lam = lam_ref[...]
        lam_full = (jnp.exp(jnp.sum(lam[0:1] * lam[1:2], axis=-1, keepdims=True))
                    - jnp.exp(jnp.sum(lam[2:3] * lam[3:4], axis=-1, keepdims=True)) + lam_init)
        a0, a1 = acc_scr[0], acc_scr[1]
        o = (a0[:, :DIFF_DV] / a0[:, DIFF_DV:DIFF_DV + 1]
             - lam_full * (a1[:, :DIFF_DV] / a1[:, DIFF_DV:DIFF_DV + 1]))
        o_ref[rows_of(qi, tq), :] = (_rms(o) * nw_ref[...] * (1.0 - lam_init)).astype(BF16)

    def reset_max():
        m_scr[...] = jnp.full(m_scr.shape, -jnp.inf, F32)

    def region(n, slot, last=False):
        qi, t = split(n)
        _when(t == 0, reset_max)
        if not last:
            scores(n + 1, 1 - slot)
        consume(n, slot)
        _when(t == n_k - 1, lambda: finalize(qi))

    acc_scr[...] = jnp.zeros_like(acc_scr)
    scores(0, 0)
    n_pair = (n_steps - 1) // 2

    def body(u, carry):
        region(2 * u, 0)
        region(2 * u + 1, 1)
        return carry

    if n_pair > 0:
        lax.fori_loop(0, n_pair, body, 0)
    for n in range(2 * n_pair, n_steps):
        region(n, n % 2, last=n == n_steps - 1)


def _attn_scratch(tq, tk):
    return ([pltpu.VMEM((2, tq, tk), F32)] * 2 + [pltpu.VMEM((2, tq, 128), F32)] * 2
            + [pltpu.VMEM((2, tq, 2 * DIFF_DV), F32), pltpu.VMEM((2, tq, 128), F32)])


def _attention(q, kv, lam, nw, *, bsz, seq, ctx_len, lam_init, with_ctx):
    tq = 1024 if seq % 1024 == 0 else 512
    n_keys = seq + ctx_len
    tk = 768 if n_keys % 768 == 0 else ctx_len
    rows = bsz * n_keys
    out = pl.pallas_call(
        functools.partial(_attn_kernel, tq=tq, tk=tk, n_q=seq // tq, n_k=n_keys // tk,
                          lam_init=lam_init),
        grid=(bsz, DIFF_HEADS),
        in_specs=[
            _const_spec(lam.shape),
            _const_spec(nw.shape),
            pl.BlockSpec((seq, 128), lambda b, h: (b, h)),
            pl.BlockSpec((n_keys, 128), lambda b, h: (b, h)),
            pl.BlockSpec((n_keys, 128), lambda b, h: (b, DIFF_HEADS + h)),
        ],
        out_specs=pl.BlockSpec((seq, 128), lambda b, h: (b, h)),
        out_shape=jax.ShapeDtypeStruct((rows, DIFF_HEADS * DIFF_DV), BF16),
        scratch_shapes=_attn_scratch(tq, tk),
        compiler_params=_cparams(("parallel", "parallel")),
        name="diff_attn",
    )(lam, nw, q, kv, kv)
    if not with_ctx:
        return out
    qblk = bsz * seq // ctx_len
    per_sample = n_keys // ctx_len

    def ctx_keys(col):
        return lambda b, h: (b * per_sample + per_sample - 1, col + h)

    return pl.pallas_call(
        functools.partial(_attn_kernel, tq=ctx_len, tk=ctx_len, n_q=1, n_k=1, lam_init=lam_init),
        grid=(bsz, DIFF_HEADS),
        in_specs=[
            _const_spec(lam.shape),
            _const_spec(nw.shape),
            pl.BlockSpec((ctx_len, 128), lambda b, h: (qblk + b, h)),
            pl.BlockSpec((ctx_len, 128), ctx_keys(0)),
            pl.BlockSpec((ctx_len, 128), ctx_keys(DIFF_HEADS)),
            pl.BlockSpec(memory_space=pl.ANY),
        ],
        out_specs=pl.BlockSpec((ctx_len, 128), lambda b, h: (qblk + b, h)),
        out_shape=jax.ShapeDtypeStruct((rows, DIFF_HEADS * DIFF_DV), BF16),
        scratch_shapes=_attn_scratch(ctx_len, ctx_len),
        input_output_aliases={5: 0},
        compiler_params=_cparams(("parallel", "parallel")),
        name="diff_attn_ctx",
    )(lam, nw, q, kv, kv, out)


def _seq_maps(bsz, nsx, nsc):
    cbase = bsz * nsx

    def fwd(b, s):
        return jnp.where(s < nsc, cbase + b * nsc + s, b * nsx + (s - nsc))

    def bwd(b, s):
        return jnp.where(s < nsc, cbase + b * nsc + (nsc - 1 - s), b * nsx + (nsx - 1 - (s - nsc)))

    return fwd, bwd


def _gdn_kernel(cvf_ref, gbf_ref, gtf_ref, cvb_ref, gbb_ref, gtb_ref, of_ref, ob_ref, st_ref):
    @pl.when(pl.program_id(1) == 0)
    def _():
        st_ref[...] = jnp.zeros_like(st_ref)

    c_ = GDN_CHUNK
    nck = cvf_ref.shape[0] // c_
    ii = lax.broadcasted_iota(jnp.int32, (c_, c_), 0)
    jj = lax.broadcasted_iota(jnp.int32, (c_, c_), 1)
    eye = (ii == jj).astype(F32)
    blk16 = (ii // 16) == (jj // 16)
    same32 = (ii // 32) == (jj // 32)
    lvl32 = same32 & jnp.logical_not(blk16)
    lvl64 = jnp.logical_not(same32)
    dirs = (
        (cvf_ref, gbf_ref, gtf_ref, of_ref, ii >= jj, ii > jj, c_ - 1),
        (cvb_ref, gbb_ref, gtb_ref, ob_ref, ii <= jj, ii < jj, 0),
    )

    class _P:
        pass

    probs = []
    for d, (cv_ref, gb_ref, gt_ref, o_ref, incl, strict, last_row) in enumerate(dirs):
        for c in range(nck):
            cc = c if d == 0 else nck - 1 - c
            for h in range(GDN_HEADS):
                p = _P()
                p.d, p.h, p.c, p.cc = d, h, c, cc
                p.cv, p.gb, p.gt, p.o = cv_ref, gb_ref, gt_ref, o_ref
                p.incl, p.strict, p.last_row = incl, strict, last_row
                p.rows = slice(cc * c_, (cc + 1) * c_)
                p.idx = d * GDN_HEADS + h
                probs.append(p)

    def q_of(p):
        return p.cv[p.rows, p.h * GDN_DK:(p.h + 1) * GDN_DK]

    def k_of(p):
        return p.cv[p.rows, 512 + p.h * GDN_DK:512 + (p.h + 1) * GDN_DK]

    def v_of(p):
        return p.cv[p.rows, 1024 + p.h * GDN_DV:1024 + (p.h + 1) * GDN_DV]

    def gcol_of(p):
        return p.gb[p.rows, :][:, LANE_GA + p.idx:LANE_GA + p.idx + 1]

    def beta_of(p):
        return p.gb[p.rows, :][:, LANE_GB + p.idx:LANE_GB + p.idx + 1]

    def glast_of(p):
        r = p.last_row
        return p.gb[p.rows, :][r:r + 1, LANE_GA + p.idx:LANE_GA + p.idx + 1]

    for p in probs:
        kh = k_of(p)
        grow = p.gt[p.cc][p.idx:p.idx + 1, :]
        dec = jnp.where(p.incl, jnp.exp(jnp.minimum(gcol_of(p) - grow, 0.0)), 0.0)
        ga = _bdot_nt(jnp.concatenate([kh * beta_of(p), q_of(p)], axis=0), kh)
        p.m = jnp.where(p.strict, ga[:c_] * dec, 0.0)
        p.aqk = ga[c_:] * dec
    for p in probs:
        mb = jnp.where(blk16, p.m, 0.0)
        p.tinv = eye - mb
        p.sq = _bdot(mb, mb)
    for step in range(3):
        for p in probs:
            p.tinv = p.tinv + _bdot(p.tinv, p.sq)
            if step < 2:
                p.sq = _bdot(p.sq, p.sq)
    for lvl in (lvl32, lvl64):
        for p in probs:
            p.x = _bdot(jnp.where(lvl, p.m, 0.0), p.tinv)
        for p in probs:
            p.tinv = p.tinv - _bdot(p.tinv, p.x)
    for p in probs:
        beta = beta_of(p)
        rhs = jnp.concatenate([v_of(p) * beta, k_of(p) * (beta * jnp.exp(gcol_of(p)))], axis=1)
        p.uw = _bdot(p.tinv, rhs)
    states = [[st_ref[d, h] for h in range(GDN_HEADS)] for d in range(2)]
    for c in range(nck):
        cur = [p for p in probs if p.c == c]
        for p in cur:
            lhs = jnp.concatenate([p.uw[:, GDN_DV:], q_of(p) * jnp.exp(gcol_of(p))], axis=0)
            p.wq = _bdot(lhs, states[p.d][p.h])
        for p in cur:
            p.v_new = p.uw[:, :GDN_DV] - p.wq[:c_]
            glast = glast_of(p)
            k_dec = k_of(p) * jnp.exp(glast - gcol_of(p))
            states[p.d][p.h] = (states[p.d][p.h] * jnp.exp(glast)
                                + _bdot_tn(k_dec, p.v_new))
        for p in cur:
            p.out = p.wq[c_:] + _bdot(p.aqk, p.v_new)
    for p in probs:
        p.o[p.rows, p.h * GDN_DV:(p.h + 1) * GDN_DV] = p.out
    for d in range(2):
        for h in range(GDN_HEADS):
            st_ref[d, h] = states[d][h]


def _gdn(cv, gb, gt, *, bsz, nsx, nsc):
    t = SEQ_T
    rows = cv.shape[0]
    fwd, bwd = _seq_maps(bsz, nsx, nsc)
    nck = t // GDN_CHUNK

    def specs(fn):
        return [
            pl.BlockSpec((t, GDN_QKV), lambda b, s: (fn(b, s), 0)),
            pl.BlockSpec((t, 128), lambda b, s: (fn(b, s), 0)),
            pl.BlockSpec((nck, 16, GDN_CHUNK), lambda b, s: (fn(b, s), 0, 0)),
        ]

    return pl.pallas_call(
        _gdn_kernel,
        grid=(bsz, nsc + nsx),
        in_specs=specs(fwd) + specs(bwd),
        out_specs=[pl.BlockSpec((t, 512), lambda b, s: (fwd(b, s), 0)),
                   pl.BlockSpec((t, 512), lambda b, s: (bwd(b, s), 0))],
        out_shape=[jax.ShapeDtypeStruct((rows, 512), F32)] * 2,
        scratch_shapes=[pltpu.VMEM((2, GDN_HEADS, GDN_DK, GDN_DV), F32)],
        compiler_params=_cparams(("parallel", "arbitrary")),
        name="gdn_scan",
    )(cv, gb, gt, cv, gb, gt)


def _ssd_kernel(dsk_ref, cvf_ref, gbf_ref, dtf_ref, atf_ref, cvb_ref, gbb_ref, dtb_ref, atb_ref,
                yf_ref, yb_ref, st_ref):
    @pl.when(pl.program_id(1) == 0)
    def _():
        st_ref[...] = jnp.zeros_like(st_ref)

    c_ = SSD_CHUNK
    p_ = SSD_HEADDIM
    hpg = SSD_HEADS // SSD_GROUPS
    nck = cvf_ref.shape[0] // c_
    ii = lax.broadcasted_iota(jnp.int32, (c_, c_), 0)
    jj = lax.broadcasted_iota(jnp.int32, (c_, c_), 1)
    dirs = (
        (cvf_ref, gbf_ref, dtf_ref, atf_ref, yf_ref, ii >= jj, c_ - 1),
        (cvb_ref, gbb_ref, dtb_ref, atb_ref, yb_ref, ii <= jj, 0),
    )

    states = [[st_ref[d, g] for g in range(SSD_GROUPS)] for d in range(2)]
    stores = []
    for d, (cv_ref, gb_ref, dt_ref, at_ref, y_ref, incl, last_row) in enumerate(dirs):
        for c in range(nck):
            cc = c if d == 0 else nck - 1 - c
            rows = slice(cc * c_, (cc + 1) * c_)
            acs = gb_ref[rows, :]
            dts = dt_ref[rows, :]
            at = at_ref[cc]
            for g in range(SSD_GROUPS):
                bm = cv_ref[rows, SSD_INNER + g * SSD_STATE:SSD_INNER + (g + 1) * SSD_STATE]
                cm = cv_ref[rows, SSD_INNER + (SSD_GROUPS + g) * SSD_STATE:
                            SSD_INNER + (SSD_GROUPS + g + 1) * SSD_STATE]
                cb = _bdot_nt(cm, bm)
                s_old = states[d][g]
                y_off = _bdot(cm, s_old)
                xd, sdec, ys = [], [], []
                for hh in range(hpg):
                    h = g * hpg + hh
                    idx = d * SSD_HEADS + h
                    ln = LANE_DT + idx
                    acol = acs[:, ln:ln + 1]
                    arow = at[idx:idx + 1, :]
                    alast = acs[last_row:last_row + 1, ln:ln + 1]
                    xh = cv_ref[rows, h * p_:(h + 1) * p_]
                    xdt = xh * dts[:, ln:ln + 1]
                    ldec = jnp.where(incl, jnp.exp(jnp.minimum(acol - arow, 0.0)), 0.0)
                    y = _bdot(cb * ldec, xdt) + y_off[:, hh * p_:(hh + 1) * p_] * jnp.exp(acol)
                    if d == 0:
                        y = y + dsk_ref[:, h * p_:(h + 1) * p_] * xh
                    ys.append(y)
                    xd.append(xdt * jnp.exp(alast - acol))
                    sdec.append(jnp.broadcast_to(jnp.exp(alast), (1, p_)))
                stores.append((y_ref, rows, slice(g * hpg * p_, (g + 1) * hpg * p_),
                               jnp.concatenate(ys, axis=1)))
                states[d][g] = (s_old * jnp.concatenate(sdec, axis=1)
                                + _bdot_tn(bm, jnp.concatenate(xd, axis=1)))
    for y_ref, rows, cols, val in stores:
        y_ref[rows, cols] = val
    for d in range(2):
        for g in range(SSD_GROUPS):
            st_ref[d, g] = states[d][g]


def _ssd(dsk, cv, gb, dt, at, *, bsz, nsx, nsc):
    t = SEQ_T
    rows = cv.shape[0]
    fwd, bwd = _seq_maps(bsz, nsx, nsc)
    nck = t // SSD_CHUNK

    def specs(fn):
        return [
            pl.BlockSpec((t, SSD_XBC), lambda b, s: (fn(b, s), 0)),
            pl.BlockSpec((t, 128), lambda b, s: (fn(b, s), 0)),
            pl.BlockSpec((t, 128), lambda b, s: (fn(b, s), 0)),
            pl.BlockSpec((nck, 16, SSD_CHUNK), lambda b, s: (fn(b, s), 0, 0)),
        ]

    return pl.pallas_call(
        _ssd_kernel,
        grid=(bsz, nsc + nsx),
        in_specs=[_const_spec(dsk.shape)] + specs(fwd) + specs(bwd),
        out_specs=[pl.BlockSpec((t, SSD_INNER), lambda b, s: (fwd(b, s), 0)),
                   pl.BlockSpec((t, SSD_INNER), lambda b, s: (bwd(b, s), 0))],
        out_shape=[jax.ShapeDtypeStruct((rows, SSD_INNER), F32)] * 2,
        scratch_shapes=[pltpu.VMEM((2, SSD_GROUPS, SSD_STATE, SSD_INNER // SSD_GROUPS), F32)],
        compiler_params=_cparams(("parallel", "arbitrary")),
        name="ssd_scan",
    )(dsk, cv, gb, dt, at, cv, gb, dt, at)


def _merge_kernel(x_ref, mod_ref, ng_ref, gof_ref, gob_ref, gz_ref, att_ref, syf_ref, syb_ref,
                  sz_ref, gate_ref, gnw_ref, snw_ref, wb_ref, wo_ref, o_ref):
    oa = gof_ref[...] + gob_ref[...]
    parts = []
    for h in range(GDN_HEADS):
        hs = slice(h * GDN_DV, (h + 1) * GDN_DV)
        parts.append(_rms(oa[:, hs]) * gnw_ref[:, hs])
    oa = jnp.concatenate(parts, axis=1) * _silu(gz_ref[...].astype(F32))
    oc = (syf_ref[...] + syb_ref[...]) * _silu(sz_ref[...].astype(F32))
    oc = _rms(oc) * snw_ref[...]
    outs = (oa.astype(BF16), att_ref[...], oc.astype(BF16))
    y = None
    for n in range(N_BRANCH):
        gate = jax.nn.sigmoid(gate_ref[:, n * D_MODEL:(n + 1) * D_MODEL].astype(F32))
        term = gate * jnp.dot(outs[n], wb_ref[n], preferred_element_type=F32)
        y = term if y is None else y + term
    y = jnp.dot(y.astype(BF16), wo_ref[...], preferred_element_type=F32)
    m = mod_ref[0]
    o_ref[...] = x_ref[...] + m[5:6] * (_rms(y) * ng_ref[3:4, :])


def _merge(s, mod, ng, gof, gob, hh, att, syf, syb, gnw, snw, wb, wo, *, tm, n_tiles,
           tiles_per_sample, n_samples):
    rows = n_tiles * tm
    row512 = pl.BlockSpec((tm, 512), lambda i: (i, 0))
    return pl.pallas_call(
        _merge_kernel,
        grid=(n_tiles,),
        in_specs=[
            pl.BlockSpec((tm, D_MODEL), lambda i: (i, 0)),
            pl.BlockSpec((1, N_MOD, D_MODEL),
                         lambda i: (jnp.minimum(i // tiles_per_sample, n_samples), 0, 0)),
            _const_spec(ng.shape),
            row512, row512,
            pl.BlockSpec((tm, 512), lambda i: (i, H_GZ // 512)),
            row512, row512, row512,
            pl.BlockSpec((tm, 512), lambda i: (i, H_SZ // 512)),
            pl.BlockSpec((tm, N_BRANCH * D_MODEL), lambda i: (i, 0)),
            _const_spec(gnw.shape),
            _const_spec(snw.shape),
            _const_spec(wb.shape),
            _const_spec(wo.shape),
        ],
        out_specs=pl.BlockSpec((tm, D_MODEL), lambda i: (i, 0)),
        out_shape=jax.ShapeDtypeStruct((rows, D_MODEL), F32),
        compiler_params=_cparams(("parallel",)),
        name="merge",
    )(s, mod, ng, gof, gob, hh, att, syf, syb, hh, hh, gnw, snw, wb, wo)


def _deinterleave_pairs(w):
    d = w.shape[0]
    w = w.reshape(d, DIFF_HEADS, 2, DIFF_DQK // 2, 2)
    return jnp.swapaxes(w, 3, 4).reshape(d, DIFF_HEADS * 2 * DIFF_DQK)


def _proj_weight(w_in):
    o = 0
    parts = {}
    for name, width in (("gqkv", GDN_QKV), ("gz", 512), ("ga", 8), ("gb", 8), ("dq", 512),
                        ("dk", 512), ("dv", 512), ("sz", 512), ("sxbc", SSD_XBC), ("sdt", 16),
                        ("gate", N_BRANCH * D_MODEL)):
        parts[name] = w_in[:, o:o + width]
        o += width
    small_pad = jnp.zeros((w_in.shape[0], SMALL_W - 32), w_in.dtype)
    cols = [parts["gqkv"], parts["sxbc"], parts["ga"], parts["gb"], parts["sdt"], small_pad,
            _deinterleave_pairs(parts["dq"]), _deinterleave_pairs(parts["dk"]),
            parts["gate"], parts["dv"], parts["gz"], parts["sz"]]
    return jnp.concatenate(cols, axis=1).astype(BF16)


def _rope_tables(seq):
    rows = seq // GRID_W
    row = jnp.repeat(jnp.arange(rows, dtype=F32), GRID_W)
    col = jnp.tile(jnp.arange(GRID_W, dtype=F32), rows)
    inv = ROPE_BASE ** (-jnp.arange(ROPE_PAIRS_PER_AXIS, dtype=F32) / ROPE_PAIRS_PER_AXIS)
    ang = jnp.concatenate([row[:, None] * inv, col[:, None] * inv], axis=-1)
    cos, sin = jnp.cos(ang), jnp.sin(ang)
    zero = jnp.zeros_like(sin)
    cos = jnp.tile(cos, (1, 4))
    sinp = jnp.tile(jnp.concatenate([zero, sin], axis=1), (1, 2))
    sinm = jnp.tile(jnp.concatenate([-sin, zero], axis=1), (1, 2))
    return cos, sinp, sinm


def _lane_vec(pieces):
    v = jnp.zeros((128,), F32)
    for off, val in pieces:
        v = lax.dynamic_update_slice(v, val.reshape(-1).astype(F32), (off,))
    return v.reshape(1, 128)


def kernel(x, c, ctx, c_ctx, w_ada, b_ada, norm_g, w_ffn_in, w_ffn_out, w_in, gdn_conv_w, gdn_a_log,
           gdn_dt_bias, gdn_norm_w, diff_lambda, diff_norm_w, ssd_conv_w, ssd_conv_b, ssd_a_log,
           ssd_dt_bias, ssd_d, ssd_norm_w, w_branch, w_out):
    bsz, seq, _ = x.shape
    ctx_len = ctx.shape[1]
    t = SEQ_T
    tm = 512
    assert ctx_len == t and seq % 512 == 0 and (bsz * ctx_len) % tm == 0 and seq % GRID_W == 0
    nx_rows, nc_rows = bsz * seq, bsz * ctx_len
    n_rows = nx_rows + nc_rows

    s = jnp.concatenate([x.reshape(nx_rows, D_MODEL), ctx.reshape(nc_rows, D_MODEL)], axis=0)
    nm = -(-(bsz + 1) // 8) * 8
    cvec = jnp.zeros((nm, D_MODEL), F32).at[:bsz].set(c).at[bsz].set(c_ctx)
    mod_all = _mod_table(cvec, w_ada, b_ada)
    cos, sinp, sinm = _rope_tables(seq)

    tok = dict(tm=tm, tiles_per_sample=seq // tm, n_samples=bsz)
    all_tiles, x_tiles = n_rows // tm, nx_rows // tm
    seq_kw = dict(n_tiles=n_rows // t, nx_tiles=nx_rows // t, x_tiles_per_seq=seq // t)
    nsx, nsc = seq // t, ctx_len // t

    for i in range(DEPTH):
        last = i == DEPTH - 1
        lam_init = 0.8 - 0.6 * math.exp(-0.3 * i)
        mod, ng = mod_all[i], norm_g[i]
        w1 = [w_ffn_in[i, j].astype(BF16) for j in range(2)]
        w2 = [w_ffn_out[i, j].astype(BF16) for j in range(2)]

        s = _half_ffn(s, mod, ng, w1[0], w2[0], slot=0, gslot=0, n_tiles=all_tiles, **tok)

        f, hh = _in_proj(s, mod, ng, _proj_weight(w_in[i]), n_tiles=all_tiles, **tok)
        cw = jnp.concatenate([gdn_conv_w[i], ssd_conv_w[i]], axis=1)
        cb = jnp.concatenate([jnp.zeros((GDN_QKV,), F32), ssd_conv_b[i]]).reshape(1, CONV_W)
        alog = _lane_vec([(LANE_GA, gdn_a_log[i]), (LANE_DT, ssd_a_log[i])])
        dtb = _lane_vec([(LANE_GA, gdn_dt_bias[i]), (LANE_DT, ssd_dt_bias[i])])
        cvg, cvs, gb, dt, gt, at = _prep(f, cw, cb, alog, dtb, c_tiles_per_seq=ctx_len // t, **seq_kw)
        qr, kv = _rope(f, hh, cos, sinp, sinm, c_tiles_per_seq=ctx_len // t, **seq_kw)

        gof, gob = _gdn(cvg, gb, gt, bsz=bsz, nsx=nsx, nsc=nsc)
        dsk = jnp.repeat(ssd_d[i].astype(F32), SSD_HEADDIM).reshape(1, SSD_INNER)
        syf, syb = _ssd(dsk, cvs, gb, dt, at, bsz=bsz, nsx=nsx, nsc=nsc)
        att = _attention(qr, kv, diff_lambda[i], diff_norm_w[i].reshape(1, DIFF_DV), bsz=bsz,
                         seq=seq, ctx_len=ctx_len, lam_init=lam_init, with_ctx=not last)

        n_out = x_tiles if last else all_tiles
        gnw = jnp.tile(gdn_norm_w[i], GDN_HEADS).reshape(1, GDN_HEADS * GDN_DV)
        snw = ssd_norm_w[i].reshape(1, SSD_INNER)
        s = _merge(s, mod, ng, gof, gob, hh, att, syf, syb, gnw, snw, w_branch[i].astype(BF16),
                   w_out[i].astype(BF16), n_tiles=n_out, **tok)
        s = _half_ffn(s, mod, ng, w1[1], w2[1], slot=6, gslot=4, n_tiles=n_out, **tok)

    return s[:nx_rows].reshape(bsz, seq, D_MODEL)
```

```python
import functools
import math

import jax
import jax.numpy as jnp
from jax import lax
from jax.experimental import pallas as pl
from jax.experimental.pallas import tpu as pltpu

F32 = jnp.float32
BF16 = jnp.bfloat16

D_MODEL = 1024
DEPTH = 2
GRID_W = 64
EPS = 1e-6
N_MOD = 9
D_FF = 2816
CONV_K = 5
GDN_HEADS = 4
GDN_DK = 128
GDN_DV = 128
GDN_CHUNK = 64
DIFF_HEADS = 4
DIFF_DQK = 64
DIFF_DV = 128
ROPE_BASE = 10000.0
ROPE_PAIRS_PER_AXIS = DIFF_DQK // 4
SSD_HEADS = 8
SSD_HEADDIM = 64
SSD_INNER = SSD_HEADS * SSD_HEADDIM
SSD_GROUPS = 2
SSD_STATE = 128
SSD_CHUNK = 128
N_BRANCH = 3
BRANCH_W = 512

GDN_QKV = 2 * GDN_HEADS * GDN_DK + GDN_HEADS * GDN_DV
SSD_XBC = SSD_INNER + 2 * SSD_GROUPS * SSD_STATE
CONV_W = GDN_QKV + SSD_XBC
SMALL_W = 512
F_SMALL = CONV_W
F_DQ = F_SMALL + SMALL_W
F_W = F_DQ + 2 * 512
H_DV = N_BRANCH * D_MODEL
H_GZ = H_DV + 512
H_SZ = H_GZ + 512
H_W = H_SZ + 512
LANE_GA = 0
LANE_GB = 8
LANE_DT = 16

Q_SCALE = DIFF_DQK ** -0.5 * math.log2(math.e)

FF_CHUNK = 256
PROJ_TN = 512
SEQ_T = 256
VMEM_LIMIT = 56 * 1024 * 1024


def _cparams(sem):
    return pltpu.CompilerParams(dimension_semantics=sem, vmem_limit_bytes=VMEM_LIMIT)


def _const_spec(shape):
    nd = len(shape)
    return pl.BlockSpec(shape, lambda *_: (0,) * nd, pipeline_mode=pl.Buffered(1))


def _silu(v):
    return v * jax.nn.sigmoid(v)


def _bdot(a, b):
    return jnp.dot(a.astype(BF16), b.astype(BF16), preferred_element_type=F32)


def _bdot_nt(a, b):
    return lax.dot_general(a.astype(BF16), b.astype(BF16), (((1,), (1,)), ((), ())),
                           preferred_element_type=F32)


def _bdot_tn(a, b):
    return lax.dot_general(a.astype(BF16), b.astype(BF16), (((0,), (0,)), ((), ())),
                           preferred_element_type=F32)


def _rms(v):
    return v * lax.rsqrt(jnp.mean(v * v, axis=-1, keepdims=True) + EPS)


def _mod_kernel(c_ref, w_ref, b_ref, o_ref):
    sc = _silu(c_ref[...])
    o_ref[0, 0] = jnp.dot(sc, w_ref[0], precision=lax.Precision.HIGHEST,
                          preferred_element_type=F32) + b_ref[0]


def _mod_table(cvec, w_ada, b_ada):
    nm = cvec.shape[0]
    out = pl.pallas_call(
        _mod_kernel,
        grid=(DEPTH, N_MOD),
        in_specs=[
            pl.BlockSpec((nm, D_MODEL), lambda l, j: (0, 0)),
            pl.BlockSpec((1, D_MODEL, D_MODEL), lambda l, j: (l, 0, j)),
            pl.BlockSpec((1, 1, D_MODEL), lambda l, j: (l, 0, j)),
        ],
        out_specs=pl.BlockSpec((1, 1, nm, D_MODEL), lambda l, j: (l, j, 0, 0)),
        out_shape=jax.ShapeDtypeStruct((DEPTH, N_MOD, nm, D_MODEL), F32),
        compiler_params=_cparams(("parallel", "parallel")),
        name="mod_table",
    )(cvec, w_ada, b_ada.reshape(DEPTH, 1, N_MOD * D_MODEL))
    return jnp.transpose(out, (0, 2, 1, 3))


def _ffn_kernel(x_ref, mod_ref, ng_ref, w1_ref, w2_ref, *rest, slot, gslot):
    o_ref = rest[-1]
    x = x_ref[...]
    m = mod_ref[0]
    shift, scale, gate = m[slot:slot + 1], m[slot + 1:slot + 2], m[slot + 2:slot + 3]
    h = _rms(x) * ng_ref[gslot:gslot + 1, :]
    hb = (h * (1.0 + scale) + shift).astype(BF16)
    acc = None
    for c in range(D_FF // FF_CHUNK):
        lo = c * FF_CHUNK
        g = jnp.dot(hb, w1_ref[:, lo:lo + FF_CHUNK], preferred_element_type=F32)
        u = jnp.dot(hb, w1_ref[:, D_FF + lo:D_FF + lo + FF_CHUNK], preferred_element_type=F32)
        a = (_silu(g) * u).astype(BF16)
        part = jnp.dot(a, w2_ref[lo:lo + FF_CHUNK, :], preferred_element_type=F32)
        acc = part if acc is None else acc + part
    y = _rms(acc) * ng_ref[gslot + 1:gslot + 2, :]
    o_ref[...] = x + 0.5 * gate * y


def _half_ffn(s, mod, ng, w1, w2, *, slot, gslot, tm, n_tiles, tiles_per_sample, n_samples,
              out_rows=None, out_tile0=0, into=None):
    rows = n_tiles * tm if out_rows is None else out_rows
    in_specs = [
        pl.BlockSpec((tm, D_MODEL), lambda i: (i, 0)),
        pl.BlockSpec((1, N_MOD, D_MODEL),
                     lambda i: (jnp.minimum((i + out_tile0) // tiles_per_sample, n_samples), 0, 0)),
        _const_spec(ng.shape),
        _const_spec(w1.shape),
        _const_spec(w2.shape),
    ]
    args = [s, mod, ng, w1, w2]
    aliases = {}
    if into is not None:
        in_specs.append(pl.BlockSpec(memory_space=pl.ANY))
        args.append(into)
        aliases = {5: 0}
    return pl.pallas_call(
        functools.partial(_ffn_kernel, slot=slot, gslot=gslot),
        grid=(n_tiles,),
        in_specs=in_specs,
        out_specs=pl.BlockSpec((tm, D_MODEL), lambda i: (i + out_tile0, 0)),
        out_shape=jax.ShapeDtypeStruct((rows, D_MODEL), F32),
        input_output_aliases=aliases,
        compiler_params=_cparams(("parallel",)),
        name="half_ffn",
    )(*args)


def _inproj_kernel(x_ref, mod_ref, ng_ref, w_ref, of_ref, oh_ref):
    m = mod_ref[0]
    h = _rms(x_ref[...]) * ng_ref[2:3, :]
    hb = (h * (1.0 + m[4:5]) + m[3:4]).astype(BF16)
    for j in range(F_W // PROJ_TN):
        cs = slice(j * PROJ_TN, (j + 1) * PROJ_TN)
        of_ref[:, cs] = jnp.dot(hb, w_ref[:, cs], preferred_element_type=F32)
    for j in range(H_W // PROJ_TN):
        cs = slice(j * PROJ_TN, (j + 1) * PROJ_TN)
        ws = slice(F_W + j * PROJ_TN, F_W + (j + 1) * PROJ_TN)
        oh_ref[:, cs] = jnp.dot(hb, w_ref[:, ws], preferred_element_type=F32).astype(BF16)


def _in_proj(s, mod, ng, w, *, tm, n_tiles, tiles_per_sample, n_samples):
    rows = n_tiles * tm
    return pl.pallas_call(
        _inproj_kernel,
        grid=(n_tiles,),
        in_specs=[
            pl.BlockSpec((tm, D_MODEL), lambda i: (i, 0)),
            pl.BlockSpec((1, N_MOD, D_MODEL),
                         lambda i: (jnp.minimum(i // tiles_per_sample, n_samples), 0, 0)),
            _const_spec(ng.shape),
            _const_spec(w.shape),
        ],
        out_specs=[
            pl.BlockSpec((tm, F_W), lambda i: (i, 0)),
            pl.BlockSpec((tm, H_W), lambda i: (i, 0)),
        ],
        out_shape=[jax.ShapeDtypeStruct((rows, F_W), F32),
                   jax.ShapeDtypeStruct((rows, H_W), BF16)],
        compiler_params=_cparams(("parallel",)),
        name="in_proj",
    )(s, mod, ng, w)


def _prep_kernel(x_ref, prev_ref, next_ref, small_ref, cw_ref, cb_ref, alog_ref, dtb_ref,
                 cvg_ref, cvs_ref, gb_ref, dt_ref, gt_ref, at_ref, *, nx_tiles,
                 x_tiles_per_seq, c_tiles_per_seq):
    i = pl.program_id(0)
    t = x_ref.shape[0]
    pos = jnp.where(i < nx_tiles, i % x_tiles_per_seq, (i - nx_tiles) % c_tiles_per_seq)
    last = jnp.where(i < nx_tiles, x_tiles_per_seq - 1, c_tiles_per_seq - 1)
    has_prev = (pos != 0).astype(F32)
    has_next = (pos != last).astype(F32)
    pad = CONV_K // 2

    for cc in range(CONV_W // 128):
        cs = slice(cc * 128, (cc + 1) * 128)
        ext = jnp.concatenate([prev_ref[:, cs] * has_prev, x_ref[:, cs],
                               next_ref[:, cs] * has_next], axis=0)
        y = cb_ref[:, cs] + cw_ref[pad:pad + 1, cs] * ext[8:8 + t]
        for k in range(CONV_K):
            if k != pad:
                shifted = pltpu.roll(ext, (pad - k) % (t + 16), 0)[8:8 + t]
                y = y + cw_ref[k:k + 1, cs] * shifted
        y = _silu(y)
        if cc < 2 * GDN_HEADS:
            y = y * lax.rsqrt(jnp.sum(y * y, axis=-1, keepdims=True) + EPS)
            if cc < GDN_HEADS:
                y = y * (GDN_DK ** -0.5)
        if cc < GDN_QKV // 128:
            cvg_ref[:, cs] = y
        else:
            cvs_ref[:, cc * 128 - GDN_QKV:(cc + 1) * 128 - GDN_QKV] = y

    sm = small_ref[...]
    lane = lax.broadcasted_iota(jnp.int32, (t, 128), 1)
    row = lax.broadcasted_iota(jnp.int32, (t, 128), 0)
    z = sm + dtb_ref[...]
    sp = jnp.maximum(z, 0.0) + jnp.log1p(jnp.exp(-jnp.abs(z)))
    g = -jnp.exp(alog_ref[...]) * sp
    is_gdn = lane < LANE_GB
    is_ssd = (lane >= LANE_DT) & (lane < LANE_DT + 2 * SSD_HEADS)
    win = jnp.where(is_gdn, GDN_CHUNK, SSD_CHUNK)
    rmod = row & (win - 1)
    g = jnp.where(is_gdn | is_ssd, g, 0.0)
    pre, suf = g, g
    sft = 1
    while sft < SSD_CHUNK:
        pre = pre + jnp.where(rmod >= sft, pltpu.roll(pre, sft, 0), 0.0)
        suf = suf + jnp.where(rmod < win - sft, pltpu.roll(suf, t - sft, 0), 0.0)
        sft *= 2
    backward = (is_gdn & (lane >= GDN_HEADS)) | (is_ssd & (lane >= LANE_DT + SSD_HEADS))
    cum = jnp.where(backward, suf, pre)
    is_beta = (lane >= LANE_GB) & (lane < LANE_DT)
    gb = jnp.where(is_beta, jax.nn.sigmoid(sm), cum)
    gb_ref[...] = gb
    dt_ref[...] = sp
    gbt = gb.T
    for c in range(t // GDN_CHUNK):
        gt_ref[c] = gbt[0:16, c * GDN_CHUNK:(c + 1) * GDN_CHUNK]
    for c in range(t // SSD_CHUNK):
        at_ref[c] = gbt[16:32, c * SSD_CHUNK:(c + 1) * SSD_CHUNK]


def _prep(f, cw, cb, alog, dtb, *, n_tiles, nx_tiles, x_tiles_per_seq, c_tiles_per_seq):
    t = SEQ_T
    rows = n_tiles * t
    r8 = rows // 8
    t8 = t // 8
    return pl.pallas_call(
        functools.partial(_prep_kernel, nx_tiles=nx_tiles, x_tiles_per_seq=x_tiles_per_seq,
                          c_tiles_per_seq=c_tiles_per_seq),
        grid=(n_tiles,),
        in_specs=[
            pl.BlockSpec((t, CONV_W), lambda i: (i, 0)),
            pl.BlockSpec((8, CONV_W), lambda i: (jnp.maximum(i * t8 - 1, 0), 0)),
            pl.BlockSpec((8, CONV_W), lambda i: (jnp.minimum((i + 1) * t8, r8 - 1), 0)),
            pl.BlockSpec((t, 128), lambda i: (i, F_SMALL // 128)),
            _const_spec(cw.shape),
            _const_spec(cb.shape),
            _const_spec(alog.shape),
            _const_spec(dtb.shape),
        ],
        out_specs=[
            pl.BlockSpec((t, GDN_QKV), lambda i: (i, 0)),
            pl.BlockSpec((t, SSD_XBC), lambda i: (i, 0)),
            pl.BlockSpec((t, 128), lambda i: (i, 0)),
            pl.BlockSpec((t, 128), lambda i: (i, 0)),
            pl.BlockSpec((t // GDN_CHUNK, 16, GDN_CHUNK), lambda i: (i, 0, 0)),
            pl.BlockSpec((t // SSD_CHUNK, 16, SSD_CHUNK), lambda i: (i, 0, 0)),
        ],
        out_shape=[
            jax.ShapeDtypeStruct((rows, GDN_QKV), F32),
            jax.ShapeDtypeStruct((rows, SSD_XBC), F32),
            jax.ShapeDtypeStruct((rows, 128), F32),
            jax.ShapeDtypeStruct((rows, 128), F32),
            jax.ShapeDtypeStruct((rows // GDN_CHUNK, 16, GDN_CHUNK), F32),
            jax.ShapeDtypeStruct((rows // SSD_CHUNK, 16, SSD_CHUNK), F32),
        ],
        compiler_params=_cparams(("parallel",)),
        name="conv_prep",
    )(f, f, f, f, cw, cb, alog, dtb)


def _rope_kernel(x_ref, v_ref, cos_ref, sin_ref, q_ref, kv_ref, *, nx_tiles):
    i = pl.program_id(0)
    is_x = i < nx_tiles
    cos, sin = cos_ref[...], sin_ref[...]
    nblk = x_ref.shape[1] // 128
    for b in range(nblk):
        bs = slice(b * 128, (b + 1) * 128)
        v = x_ref[:, bs]
        r = v * cos + pltpu.roll(v, 64, 1) * sin
        r = jnp.where(is_x, r, v)
        if b < nblk // 2:
            q_ref[:, bs] = (r * Q_SCALE).astype(BF16)
        else:
            kv_ref[:, (b - nblk // 2) * 128:(b - nblk // 2 + 1) * 128] = r.astype(BF16)
    kv_ref[:, 512:] = v_ref[...]


def _rope(f, hh, cos, sin, *, n_tiles, nx_tiles, x_tiles_per_seq, c_tiles_per_seq):
    t = SEQ_T
    per_sample = x_tiles_per_seq + c_tiles_per_seq
    tab = pl.BlockSpec((t, 128), lambda i: (jnp.where(i < nx_tiles, i % x_tiles_per_seq, 0), 0))

    def kv_block(i):
        j = i - nx_tiles
        return jnp.where(i < nx_tiles,
                         (i // x_tiles_per_seq) * per_sample + i % x_tiles_per_seq,
                         (j // c_tiles_per_seq) * per_sample + x_tiles_per_seq + j % c_tiles_per_seq)

    return pl.pallas_call(
        functools.partial(_rope_kernel, nx_tiles=nx_tiles),
        grid=(n_tiles,),
        in_specs=[pl.BlockSpec((t, 1024), lambda i: (i, F_DQ // 1024)),
                  pl.BlockSpec((t, 512), lambda i: (i, H_DV // 512)), tab, tab],
        out_specs=[pl.BlockSpec((t, 512), lambda i: (i, 0)),
                   pl.BlockSpec((t, 1024), lambda i: (kv_block(i), 0))],
        out_shape=[jax.ShapeDtypeStruct((n_tiles * t, 512), BF16),
                   jax.ShapeDtypeStruct((n_tiles * t, 1024), BF16)],
        compiler_params=_cparams(("parallel",)),
        name="rope",
    )(f, hh, cos, sin)


def _when(cond, fn):
    if isinstance(cond, bool):
        if cond:
            fn()
    else:
        pl.when(cond)(fn)


def _attn_kernel(lam_ref, nw_ref, q_ref, k_ref, v_ref, *rest, tq, tk, n_q, n_k, lam_init):
    o_ref, s0, s1, t0, t1, acc_scr, m_scr = rest[-7:]
    s_scr, mt_scr = (s0, s1), (t0, t1)
    half0 = (lax.broadcasted_iota(jnp.int32, (tq, 128), 1) & (DIFF_DQK // 2)) == 0
    ones_col = (lax.broadcasted_iota(jnp.int32, (tk, 128), 1) == 0).astype(BF16)
    n_steps = n_q * n_k

    def rows_of(i, size):
        r0 = i * size
        if not isinstance(i, int):
            r0 = pl.multiple_of(r0, size)
        return pl.ds(r0, size)

    def split(n):
        return (n // n_k, n % n_k)

    def scores(n, slot):
        qi, t = split(n)
        q = q_ref[rows_of(qi, tq), :]
        zero = jnp.zeros_like(q)
        k = k_ref[rows_of(t, tk), :]
        for j, keep in enumerate((half0, jnp.logical_not(half0))):
            s = lax.dot_general(jnp.where(keep, q, zero), k, (((1,), (1,)), ((), ())),
                                preferred_element_type=F32)
            s_scr[slot][j] = s
            mt_scr[slot][j] = jnp.broadcast_to(jnp.max(s, axis=1, keepdims=True), (tq, 128))

    def consume(n, slot):
        _, t = split(n)
        v = jnp.concatenate([v_ref[rows_of(t, tk), :], ones_col], axis=1)
        m_prevs = [m_scr[0], m_scr[1]]
        accs = [acc_scr[0], acc_scr[1]]
        for j in range(2):
            m_new = jnp.maximum(m_prevs[j], mt_scr[slot][j])
            m_scr[j] = m_new
            alpha = jnp.exp2(m_prevs[j] - m_new)
            p = jnp.exp2((s_scr[slot][j]
                          - jnp.concatenate([m_new] * (tk // 128), axis=1)).astype(BF16))
            acc_scr[j] = (jnp.concatenate([alpha, alpha], axis=1) * accs[j]
                          + jnp.dot(p, v, preferred_element_type=F32))

    def finalize(qi):
        lam = lam_ref[...]
        lam_full = (jnp.exp(jnp.sum(lam[0:1] * lam[1:2], axis=-1, keepdims=True))
                    - jnp.exp(jnp.sum(lam[2:3] * lam[3:4], axis=-1, keepdims=True)) + lam_init)
        a0, a1 = acc_scr[0], acc_scr[1]
        o = (a0[:, :DIFF_DV] / a0[:, DIFF_DV:DIFF_DV + 1]
             - lam_full * (a1[:, :DIFF_DV] / a1[:, DIFF_DV:DIFF_DV + 1]))
        o_ref[rows_of(qi, tq), :] = (_rms(o) * nw_ref[...] * (1.0 - lam_init)).astype(BF16)

    def reset_max():
        m_scr[...] = jnp.full(m_scr.shape, -jnp.inf, F32)

    def region(n, slot, last=False):
        qi, t = split(n)
        _when(t == 0, reset_max)
        if not last:
            scores(n + 1, 1 - slot)
        consume(n, slot)
        _when(t == n_k - 1, lambda: finalize(qi))

    acc_scr[...] = jnp.zeros_like(acc_scr)
    scores(0, 0)
    n_pair = (n_steps - 1) // 2

    def body(u, carry):
        region(2 * u, 0)
        region(2 * u + 1, 1)
        return carry

    if n_pair > 0:
        lax.fori_loop(0, n_pair, body, 0)
    for n in range(2 * n_pair, n_steps):
        region(n, n % 2, last=n == n_steps - 1)


def _attn_scratch(tq, tk):
    return ([pltpu.VMEM((2, tq, tk), F32)] * 2 + [pltpu.VMEM((2, tq, 128), F32)] * 2
            + [pltpu.VMEM((2, tq, 2 * DIFF_DV), F32), pltpu.VMEM((2, tq, 128), F32)])


def _attention(q, kv, lam, nw, *, bsz, seq, ctx_len, lam_init, with_ctx):
    tq = 1024 if seq % 1024 == 0 else 512
    n_keys = seq + ctx_len
    tk = 768 if n_keys % 768 == 0 else ctx_len
    rows = bsz * n_keys
    out = pl.pallas_call(
        functools.partial(_attn_kernel, tq=tq, tk=tk, n_q=seq // tq, n_k=n_keys // tk,
                          lam_init=lam_init),
        grid=(bsz, DIFF_HEADS),
        in_specs=[
            _const_spec(lam.shape),
            _const_spec(nw.shape),
            pl.BlockSpec((seq, 128), lambda b, h: (b, h)),
            pl.BlockSpec((n_keys, 128), lambda b, h: (b, h)),
            pl.BlockSpec((n_keys, 128), lambda b, h: (b, DIFF_HEADS + h)),
        ],
        out_specs=pl.BlockSpec((seq, 128), lambda b, h: (b, h)),
        out_shape=jax.ShapeDtypeStruct((rows, DIFF_HEADS * DIFF_DV), BF16),
        scratch_shapes=_attn_scratch(tq, tk),
        compiler_params=_cparams(("parallel", "parallel")),
        name="diff_attn",
    )(lam, nw, q, kv, kv)
    if not with_ctx:
        return out
    qblk = bsz * seq // ctx_len
    per_sample = n_keys // ctx_len

    def ctx_keys(col):
        return lambda b, h: (b * per_sample + per_sample - 1, col + h)

    return pl.pallas_call(
        functools.partial(_attn_kernel, tq=ctx_len, tk=ctx_len, n_q=1, n_k=1, lam_init=lam_init),
        grid=(bsz, DIFF_HEADS),
        in_specs=[
            _const_spec(lam.shape),
            _const_spec(nw.shape),
            pl.BlockSpec((ctx_len, 128), lambda b, h: (qblk + b, h)),
            pl.BlockSpec((ctx_len, 128), ctx_keys(0)),
            pl.BlockSpec((ctx_len, 128), ctx_keys(DIFF_HEADS)),
            pl.BlockSpec(memory_space=pl.ANY),
        ],
        out_specs=pl.BlockSpec((ctx_len, 128), lambda b, h: (qblk + b, h)),
        out_shape=jax.ShapeDtypeStruct((rows, DIFF_HEADS * DIFF_DV), BF16),
        scratch_shapes=_attn_scratch(ctx_len, ctx_len),
        input_output_aliases={5: 0},
        compiler_params=_cparams(("parallel", "parallel")),
        name="diff_attn_ctx",
    )(lam, nw, q, kv, kv, out)


def _seq_maps(bsz, nsx, nsc):
    cbase = bsz * nsx

    def fwd(b, s):
        return jnp.where(s < nsc, cbase + b * nsc + s, b * nsx + (s - nsc))

    def bwd(b, s):
        return jnp.where(s < nsc, cbase + b * nsc + (nsc - 1 - s), b * nsx + (nsx - 1 - (s - nsc)))

    return fwd, bwd


def _gdn_kernel(cvf_ref, gbf_ref, gtf_ref, cvb_ref, gbb_ref, gtb_ref, of_ref, ob_ref, st_ref):
    @pl.when(pl.program_id(1) == 0)
    def _():
        st_ref[...] = jnp.zeros_like(st_ref)

    c_ = GDN_CHUNK
    nck = cvf_ref.shape[0] // c_
    ii = lax.broadcasted_iota(jnp.int32, (c_, c_), 0)
    jj = lax.broadcasted_iota(jnp.int32, (c_, c_), 1)
    eye = (ii == jj).astype(F32)
    blk16 = (ii // 16) == (jj // 16)
    same32 = (ii // 32) == (jj // 32)
    lvl32 = same32 & jnp.logical_not(blk16)
    lvl64 = jnp.logical_not(same32)
    dirs = (
        (cvf_ref, gbf_ref, gtf_ref, of_ref, ii >= jj, ii > jj, c_ - 1),
        (cvb_ref, gbb_ref, gtb_ref, ob_ref, ii <= jj, ii < jj, 0),
    )

    class _P:
        pass

    probs = []
    for d, (cv_ref, gb_ref, gt_ref, o_ref, incl, strict, last_row) in enumerate(dirs):
        for c in range(nck):
            cc = c if d == 0 else nck - 1 - c
            for h in range(GDN_HEADS):
                p = _P()
                p.d, p.h, p.c, p.cc = d, h, c, cc
                p.cv, p.gb, p.gt, p.o = cv_ref, gb_ref, gt_ref, o_ref
                p.incl, p.strict, p.last_row = incl, strict, last_row
                p.rows = slice(cc * c_, (cc + 1) * c_)
                p.idx = d * GDN_HEADS + h
                probs.append(p)

    def q_of(p):
        return p.cv[p.rows, p.h * GDN_DK:(p.h + 1) * GDN_DK]

    def k_of(p):
        return p.cv[p.rows, 512 + p.h * GDN_DK:512 + (p.h + 1) * GDN_DK]

    def v_of(p):
        return p.cv[p.rows, 1024 + p.h * GDN_DV:1024 + (p.h + 1) * GDN_DV]

    def gcol_of(p):
        return p.gb[p.rows, :][:, LANE_GA + p.idx:LANE_GA + p.idx + 1]

    def beta_of(p):
        return p.gb[p.rows, :][:, LANE_GB + p.idx:LANE_GB + p.idx + 1]

    def glast_of(p):
        r = p.last_row
        return p.gb[p.rows, :][r:r + 1, LANE_GA + p.idx:LANE_GA + p.idx + 1]

    for p in probs:
        kh = k_of(p)
        grow = p.gt[p.cc][p.idx:p.idx + 1, :]
        dec = jnp.where(p.incl, jnp.exp(jnp.minimum(gcol_of(p) - grow, 0.0)), 0.0)
        ga = _bdot_nt(jnp.concatenate([kh * beta_of(p), q_of(p)], axis=0), kh)
        p.m = jnp.where(p.strict, ga[:c_] * dec, 0.0)
        p.aqk = ga[c_:] * dec
    for p in probs:
        mb = jnp.where(blk16, p.m, 0.0)
        p.tinv = eye - mb
        p.sq = _bdot(mb, mb)
    for step in range(3):
        for p in probs:
            p.tinv = p.tinv + _bdot(p.tinv, p.sq)
            if step < 2:
                p.sq = _bdot(p.sq, p.sq)
    for lvl in (lvl32, lvl64):
        for p in probs:
            p.x = _bdot(jnp.where(lvl, p.m, 0.0), p.tinv)
        for p in probs:
            p.tinv = p.tinv - _bdot(p.tinv, p.x)
    for p in probs:
        beta = beta_of(p)
        rhs = jnp.concatenate([v_of(p) * beta, k_of(p) * (beta * jnp.exp(gcol_of(p)))], axis=1)
        p.uw = _bdot(p.tinv, rhs)
    states = [[st_ref[d, h] for h in range(GDN_HEADS)] for d in range(2)]
    for c in range(nck):
        cur = [p for p in probs if p.c == c]
        for p in cur:
            lhs = jnp.concatenate([p.uw[:, GDN_DV:], q_of(p) * jnp.exp(gcol_of(p))], axis=0)
            p.wq = _bdot(lhs, states[p.d][p.h])
        for p in cur:
            p.v_new = p.uw[:, :GDN_DV] - p.wq[:c_]
            glast = glast_of(p)
            k_dec = k_of(p) * jnp.exp(glast - gcol_of(p))
            states[p.d][p.h] = (states[p.d][p.h] * jnp.exp(glast)
                                + _bdot_tn(k_dec, p.v_new))
        for p in cur:
            p.out = p.wq[c_:] + _bdot(p.aqk, p.v_new)
    for p in probs:
        p.o[p.rows, p.h * GDN_DV:(p.h + 1) * GDN_DV] = p.out
    for d in range(2):
        for h in range(GDN_HEADS):
            st_ref[d, h] = states[d][h]


def _gdn(cv, gb, gt, *, bsz, nsx, nsc):
    t = SEQ_T
    rows = cv.shape[0]
    fwd, bwd = _seq_maps(bsz, nsx, nsc)
    nck = t // GDN_CHUNK

    def specs(fn):
        return [
            pl.BlockSpec((t, GDN_QKV), lambda b, s: (fn(b, s), 0)),
            pl.BlockSpec((t, 128), lambda b, s: (fn(b, s), 0)),
            pl.BlockSpec((nck, 16, GDN_CHUNK), lambda b, s: (fn(b, s), 0, 0)),
        ]

    return pl.pallas_call(
        _gdn_kernel,
        grid=(bsz, nsc + nsx),
        in_specs=specs(fwd) + specs(bwd),
        out_specs=[pl.BlockSpec((t, 512), lambda b, s: (fwd(b, s), 0)),
                   pl.BlockSpec((t, 512), lambda b, s: (bwd(b, s), 0))],
        out_shape=[jax.ShapeDtypeStruct((rows, 512), F32)] * 2,
        scratch_shapes=[pltpu.VMEM((2, GDN_HEADS, GDN_DK, GDN_DV), F32)],
        compiler_params=_cparams(("parallel", "arbitrary")),
        name="gdn_scan",
    )(cv, gb, gt, cv, gb, gt)


def _split_dot(x, sel):
    hi = x.astype(BF16)
    lo = (x - hi.astype(F32)).astype(BF16)
    return (jnp.dot(hi, sel, preferred_element_type=F32)
            + jnp.dot(lo, sel, preferred_element_type=F32))


def _ssd_kernel(dsk_ref, sel64_ref, cvf_ref, gbf_ref, dtf_ref, atf_ref, cvb_ref,
                gbb_ref, dtb_ref, atb_ref, yf_ref, yb_ref, st_ref):
    @pl.when(pl.program_id(1) == 0)
    def _():
        st_ref[...] = jnp.zeros_like(st_ref)

    c_ = SSD_CHUNK
    p_ = SSD_HEADDIM
    hpg = SSD_HEADS // SSD_GROUPS
    gw = hpg * p_
    nck = cvf_ref.shape[0] // c_
    ii = lax.broadcasted_iota(jnp.int32, (c_, c_), 0)
    jj = lax.broadcasted_iota(jnp.int32, (c_, c_), 1)
    first_half = lax.broadcasted_iota(jnp.int32, (c_, 2 * p_), 1) < p_
    dirs = (
        (cvf_ref, gbf_ref, dtf_ref, atf_ref, yf_ref, ii >= jj, c_ - 1),
        (cvb_ref, gbb_ref, dtb_ref, atb_ref, yb_ref, ii <= jj, 0),
    )

    class _P:
        pass

    probs = []
    for d, (cv_ref, gb_ref, dt_ref, at_ref, y_ref, incl, last_row) in enumerate(dirs):
        for c in range(nck):
            p = _P()
            p.d, p.c = d, c
            p.cc = c if d == 0 else nck - 1 - c
            p.rows = slice(p.cc * c_, (p.cc + 1) * c_)
            p.cv, p.y, p.incl, p.last_row = cv_ref, y_ref, incl, last_row
            p.at = at_ref[p.cc]
            p.acs, dts = gb_ref[p.rows, :], dt_ref[p.rows, :]
            p.a64 = _split_dot(p.acs, sel64_ref[d])
            xs = cv_ref[p.rows, 0:SSD_INNER]
            p.xdt = xs * _split_dot(dts, sel64_ref[d])
            alast = p.a64[last_row:last_row + 1, :]
            p.xd = p.xdt * jnp.exp(alast - p.a64)
            p.sdec = jnp.exp(alast)
            p.skip = dsk_ref[...] * xs if d == 0 else None
            probs.append(p)

    def b_of(p, g):
        return p.cv[p.rows, SSD_INNER + g * SSD_STATE:SSD_INNER + (g + 1) * SSD_STATE]

    def c_of(p, g):
        return p.cv[p.rows, SSD_INNER + (SSD_GROUPS + g) * SSD_STATE:
                    SSD_INNER + (SSD_GROUPS + g + 1) * SSD_STATE]

    states = [[st_ref[d, g] for g in range(SSD_GROUPS)] for d in range(2)]

    def chain_read(c):
        for p in probs:
            if p.c == c:
                p.y_off = [_bdot(c_of(p, g), states[p.d][g]) for g in range(SSD_GROUPS)]

    def chain_write(c):
        for p in probs:
            if p.c == c:
                for g in range(SSD_GROUPS):
                    gs = slice(g * gw, (g + 1) * gw)
                    states[p.d][g] = (states[p.d][g] * p.sdec[:, gs]
                                      + _bdot_tn(b_of(p, g), p.xd[:, gs]))

    def diag(c):
        for p in probs:
            if p.c != c:
                continue
            p.y_diag = []
            for g in range(SSD_GROUPS):
                cb = p.cb[g]
                for pair in range(hpg // 2):
                    lo_h = g * hpg + 2 * pair
                    xpair = p.xdt[:, lo_h * p_:(lo_h + 2) * p_]
                    halves = []
                    for h in (lo_h, lo_h + 1):
                        arow = p.at[p.d * SSD_HEADS + h:p.d * SSD_HEADS + h + 1, :]
                        ln = LANE_DT + p.d * SSD_HEADS + h
                        acol = p.acs[:, ln:ln + 1]
                        ldec = jnp.where(p.incl, jnp.exp(jnp.minimum(acol - arow, 0.0)), 0.0)
                        halves.append(_bdot(cb * ldec, xpair))
                    p.y_diag.append(jnp.where(first_half, halves[0], halves[1]))

    chain_read(0)
    for p in probs:
        p.cb = [_bdot_nt(c_of(p, g), b_of(p, g)) for g in range(SSD_GROUPS)]
    chain_write(0)
    diag(0)
    for c in range(1, nck):
        chain_read(c)
        diag(c)
        chain_write(c)
    for p in probs:
        y = jnp.concatenate(p.y_diag, axis=1) + jnp.concatenate(p.y_off, axis=1) * jnp.exp(p.a64)
        if p.skip is not None:
            y = y + p.skip
        p.y[p.rows, :] = y
    for d in range(2):
        for g in range(SSD_GROUPS):
            st_ref[d, g] = states[d][g]


def _ssd(dsk, sel64, cv, gb, dt, at, *, bsz, nsx, nsc):
    t = SEQ_T
    rows = cv.shape[0]
    fwd, bwd = _seq_maps(bsz, nsx, nsc)
    nck = t // SSD_CHUNK

    def specs(fn):
        return [
            pl.BlockSpec((t, SSD_XBC), lambda b, s: (fn(b, s), 0)),
            pl.BlockSpec((t, 128), lambda b, s: (fn(b, s), 0)),
            pl.BlockSpec((t, 128), lambda b, s: (fn(b, s), 0)),
            pl.BlockSpec((nck, 16, SSD_CHUNK), lambda b, s: (fn(b, s), 0, 0)),
        ]

    return pl.pallas_call(
        _ssd_kernel,
        grid=(bsz, nsc + nsx),
        in_specs=([_const_spec(dsk.shape), _const_spec(sel64.shape)]
                  + specs(fwd) + specs(bwd)),
        out_specs=[pl.BlockSpec((t, SSD_INNER), lambda b, s: (fwd(b, s), 0)),
                   pl.BlockSpec((t, SSD_INNER), lambda b, s: (bwd(b, s), 0))],
        out_shape=[jax.ShapeDtypeStruct((rows, SSD_INNER), F32)] * 2,
        scratch_shapes=[pltpu.VMEM((2, SSD_GROUPS, SSD_STATE, SSD_INNER // SSD_GROUPS), F32)],
        compiler_params=_cparams(("parallel", "arbitrary")),
        name="ssd_scan",
    )(dsk, sel64, cv, gb, dt, at, cv, gb, dt, at)


def _merge_kernel(x_ref, mod_ref, ng_ref, gof_ref, gob_ref, gz_ref, att_ref, syf_ref, syb_ref,
                  sz_ref, gate_ref, gnw_ref, snw_ref, wb_ref, wo_ref, o_ref):
    oa = gof_ref[...] + gob_ref[...]
    parts = []
    for h in range(GDN_HEADS):
        hs = slice(h * GDN_DV, (h + 1) * GDN_DV)
        parts.append(_rms(oa[:, hs]) * gnw_ref[:, hs])
    oa = jnp.concatenate(parts, axis=1) * _silu(gz_ref[...].astype(F32))
    oc = (syf_ref[...] + syb_ref[...]) * _silu(sz_ref[...].astype(F32))
    oc = _rms(oc) * snw_ref[...]
    outs = (oa.astype(BF16), att_ref[...], oc.astype(BF16))
    y = None
    for n in range(N_BRANCH):
        gate = jax.nn.sigmoid(gate_ref[:, n * D_MODEL:(n + 1) * D_MODEL].astype(F32))
        term = gate * jnp.dot(outs[n], wb_ref[n], preferred_element_type=F32)
        y = term if y is None else y + term
    y = jnp.dot(y.astype(BF16), wo_ref[...], preferred_element_type=F32)
    m = mod_ref[0]
    o_ref[...] = x_ref[...] + m[5:6] * (_rms(y) * ng_ref[3:4, :])


def _merge(s, mod, ng, gof, gob, hh, att, syf, syb, gnw, snw, wb, wo, *, tm, n_tiles,
           tiles_per_sample, n_samples):
    rows = n_tiles * tm
    row512 = pl.BlockSpec((tm, 512), lambda i: (i, 0))
    return pl.pallas_call(
        _merge_kernel,
        grid=(n_tiles,),
        in_specs=[
            pl.BlockSpec((tm, D_MODEL), lambda i: (i, 0)),
            pl.BlockSpec((1, N_MOD, D_MODEL),
                         lambda i: (jnp.minimum(i // tiles_per_sample, n_samples), 0, 0)),
            _const_spec(ng.shape),
            row512, row512,
            pl.BlockSpec((tm, 512), lambda i: (i, H_GZ // 512)),
            row512, row512, row512,
            pl.BlockSpec((tm, 512), lambda i: (i, H_SZ // 512)),
            pl.BlockSpec((tm, N_BRANCH * D_MODEL), lambda i: (i, 0)),
            _const_spec(gnw.shape),
            _const_spec(snw.shape),
            _const_spec(wb.shape),
            _const_spec(wo.shape),
        ],
        out_specs=pl.BlockSpec((tm, D_MODEL), lambda i: (i, 0)),
        out_shape=jax.ShapeDtypeStruct((rows, D_MODEL), F32),
        compiler_params=_cparams(("parallel",)),
        name="merge",
    )(s, mod, ng, gof, gob, hh, att, syf, syb, hh, hh, gnw, snw, wb, wo)


def _deinterleave_pairs(w):
    d = w.shape[0]
    w = w.reshape(d, DIFF_HEADS, 2, DIFF_DQK // 2, 2)
    return jnp.transpose(w, (0, 1, 4, 2, 3)).reshape(d, DIFF_HEADS * 2 * DIFF_DQK)


def _proj_weight(w_in):
    w_in = w_in.astype(BF16)
    o = 0
    parts = {}
    for name, width in (("gqkv", GDN_QKV), ("gz", 512), ("ga", 8), ("gb", 8), ("dq", 512),
                        ("dk", 512), ("dv", 512), ("sz", 512), ("sxbc", SSD_XBC), ("sdt", 16),
                        ("gate", N_BRANCH * D_MODEL)):
        parts[name] = w_in[:, o:o + width]
        o += width
    small_pad = jnp.zeros((w_in.shape[0], SMALL_W - 32), w_in.dtype)
    cols = [parts["gqkv"], parts["sxbc"], parts["ga"], parts["gb"], parts["sdt"], small_pad,
            _deinterleave_pairs(parts["dq"]), _deinterleave_pairs(parts["dk"]),
            parts["gate"], parts["dv"], parts["gz"], parts["sz"]]
    return jnp.concatenate(cols, axis=1)


def _rope_tables(seq):
    rows = seq // GRID_W
    row = jnp.repeat(jnp.arange(rows, dtype=F32), GRID_W)
    col = jnp.tile(jnp.arange(GRID_W, dtype=F32), rows)
    inv = ROPE_BASE ** (-jnp.arange(ROPE_PAIRS_PER_AXIS, dtype=F32) / ROPE_PAIRS_PER_AXIS)
    ang = jnp.concatenate([row[:, None] * inv, col[:, None] * inv], axis=-1)
    cos, sin = jnp.cos(ang), jnp.sin(ang)
    return jnp.tile(cos, (1, 4)), jnp.concatenate([-sin, -sin, sin, sin], axis=1)


def _ssd_selector():
    rows = jnp.arange(128)[None, :, None]
    cols = jnp.arange(SSD_INNER)[None, None, :]
    d = jnp.arange(2)[:, None, None]
    return (rows == LANE_DT + d * SSD_HEADS + cols // SSD_HEADDIM).astype(BF16)


def _lane_vec(pieces):
    v = jnp.zeros((128,), F32)
    for off, val in pieces:
        v = lax.dynamic_update_slice(v, val.reshape(-1).astype(F32), (off,))
    return v.reshape(1, 128)


def kernel(x, c, ctx, c_ctx, w_ada, b_ada, norm_g, w_ffn_in, w_ffn_out, w_in, gdn_conv_w, gdn_a_log,
           gdn_dt_bias, gdn_norm_w, diff_lambda, diff_norm_w, ssd_conv_w, ssd_conv_b, ssd_a_log,
           ssd_dt_bias, ssd_d, ssd_norm_w, w_branch, w_out):
    bsz, seq, _ = x.shape
    ctx_len = ctx.shape[1]
    t = SEQ_T
    tm = 512
    assert ctx_len == t and seq % 512 == 0 and (bsz * ctx_len) % tm == 0 and seq % GRID_W == 0
    nx_rows, nc_rows = bsz * seq, bsz * ctx_len
    n_rows = nx_rows + nc_rows

    nm = -(-(bsz + 1) // 8) * 8
    cvec = jnp.zeros((nm, D_MODEL), F32).at[:bsz].set(c).at[bsz].set(c_ctx)
    mod_all = _mod_table(cvec, w_ada, b_ada)
    cos, sin = _rope_tables(seq)
    sel64 = _ssd_selector()

    tok = dict(tm=tm, tiles_per_sample=seq // tm, n_samples=bsz)
    all_tiles, x_tiles = n_rows // tm, nx_rows // tm
    seq_kw = dict(n_tiles=n_rows // t, nx_tiles=nx_rows // t, x_tiles_per_seq=seq // t)
    nsx, nsc = seq // t, ctx_len // t

    s = None
    for i in range(DEPTH):
        last = i == DEPTH - 1
        lam_init = 0.8 - 0.6 * math.exp(-0.3 * i)
        mod, ng = mod_all[i], norm_g[i]
        w1 = [w_ffn_in[i, j].astype(BF16) for j in range(2)]
        w2 = [w_ffn_out[i, j].astype(BF16) for j in range(2)]

        if i == 0:
            s = _half_ffn(x.reshape(nx_rows, D_MODEL), mod, ng, w1[0], w2[0], slot=0, gslot=0,
                          n_tiles=x_tiles, out_rows=n_rows, **tok)
            s = _half_ffn(ctx.reshape(nc_rows, D_MODEL), mod, ng, w1[0], w2[0], slot=0, gslot=0,
                          n_tiles=all_tiles - x_tiles, out_rows=n_rows, out_tile0=x_tiles, into=s,
                          **tok)
        else:
            s = _half_ffn(s, mod, ng, w1[0], w2[0], slot=0, gslot=0, n_tiles=all_tiles, **tok)

        f, hh = _in_proj(s, mod, ng, _proj_weight(w_in[i]), n_tiles=all_tiles, **tok)
        cw = jnp.concatenate([gdn_conv_w[i], ssd_conv_w[i]], axis=1)
        cb = jnp.concatenate([jnp.zeros((GDN_QKV,), F32), ssd_conv_b[i]]).reshape(1, CONV_W)
        alog = _lane_vec([(LANE_GA, gdn_a_log[i]), (LANE_DT, ssd_a_log[i])])
        dtb = _lane_vec([(LANE_GA, gdn_dt_bias[i]), (LANE_DT, ssd_dt_bias[i])])
        cvg, cvs, gb, dt, gt, at = _prep(f, cw, cb, alog, dtb, c_tiles_per_seq=ctx_len // t, **seq_kw)
        qr, kv = _rope(f, hh, cos, sin, c_tiles_per_seq=ctx_len // t, **seq_kw)

        gof, gob = _gdn(cvg, gb, gt, bsz=bsz, nsx=nsx, nsc=nsc)
        dsk = jnp.repeat(ssd_d[i].astype(F32), SSD_HEADDIM).reshape(1, SSD_INNER)
        syf, syb = _ssd(dsk, sel64, cvs, gb, dt, at, bsz=bsz, nsx=nsx, nsc=nsc)
        att = _attention(qr, kv, diff_lambda[i], diff_norm_w[i].reshape(1, DIFF_DV), bsz=bsz,
                         seq=seq, ctx_len=ctx_len, lam_init=lam_init, with_ctx=not last)

        n_out = x_tiles if last else all_tiles
        gnw = jnp.tile(gdn_norm_w[i], GDN_HEADS).reshape(1, GDN_HEADS * GDN_DV)
        snw = ssd_norm_w[i].reshape(1, SSD_INNER)
        s = _merge(s, mod, ng, gof, gob, hh, att, syf, syb, gnw, snw, w_branch[i].astype(BF16),
                   w_out[i].astype(BF16), n_tiles=n_out, **tok)
        s = _half_ffn(s, mod, ng, w1[1], w2[1], slot=6, gslot=4, n_tiles=n_out, **tok)

    return s[:nx_rows].reshape(bsz, seq, D_MODEL)
```

```python
import functools
import math

import jax
import jax.numpy as jnp
from jax import lax
from jax.experimental import pallas as pl
from jax.experimental.pallas import tpu as pltpu

F32 = jnp.float32
BF16 = jnp.bfloat16

D_MODEL = 1024
DEPTH = 2
GRID_W = 64
EPS = 1e-6
N_MOD = 9
D_FF = 2816
CONV_K = 5
GDN_HEADS = 4
GDN_DK = 128
GDN_DV = 128
GDN_CHUNK = 64
DIFF_HEADS = 4
DIFF_DQK = 64
DIFF_DV = 128
ROPE_BASE = 10000.0
ROPE_PAIRS_PER_AXIS = DIFF_DQK // 4
SSD_HEADS = 8
SSD_HEADDIM = 64
SSD_INNER = SSD_HEADS * SSD_HEADDIM
SSD_GROUPS = 2
SSD_STATE = 128
SSD_CHUNK = 128
N_BRANCH = 3
BRANCH_W = 512

GDN_QKV = 2 * GDN_HEADS * GDN_DK + GDN_HEADS * GDN_DV
SSD_XBC = SSD_INNER + 2 * SSD_GROUPS * SSD_STATE
CONV_W = GDN_QKV + SSD_XBC
SMALL_W = 128
F_DQ = CONV_W
F_SMALL = F_DQ + 2 * 512
F_W = F_SMALL + SMALL_W
H_DV = N_BRANCH * D_MODEL
H_GZ = H_DV + 512
H_SZ = H_GZ + 512
H_W = H_SZ + 512
LANE_GA = 0
LANE_GB = 8
LANE_DT = 16

Q_SCALE = DIFF_DQK ** -0.5 * math.log2(math.e)

FF_CHUNK = 256
PROJ_TN = 512
SEQ_T = 256
VMEM_LIMIT = 56 * 1024 * 1024


def _cparams(sem):
    return pltpu.CompilerParams(dimension_semantics=sem, vmem_limit_bytes=VMEM_LIMIT)


def _const_spec(shape):
    nd = len(shape)
    return pl.BlockSpec(shape, lambda *_: (0,) * nd, pipeline_mode=pl.Buffered(1))


def _silu(v):
    return v * jax.nn.sigmoid(v)


def _bdot(a, b):
    return jnp.dot(a.astype(BF16), b.astype(BF16), preferred_element_type=F32)


def _bdot_nt(a, b):
    return lax.dot_general(a.astype(BF16), b.astype(BF16), (((1,), (1,)), ((), ())),
                           preferred_element_type=F32)


def _bdot_tn(a, b):
    return lax.dot_general(a.astype(BF16), b.astype(BF16), (((0,), (0,)), ((), ())),
                           preferred_element_type=F32)


def _rms(v):
    return v * lax.rsqrt(jnp.mean(v * v, axis=-1, keepdims=True) + EPS)


def _mod_kernel(c_ref, w_ref, b_ref, o_ref):
    sc = _silu(c_ref[...])
    o_ref[0, 0] = jnp.dot(sc, w_ref[0], precision=lax.Precision.HIGHEST,
                          preferred_element_type=F32) + b_ref[0]


def _mod_table(cvec, w_ada, b_ada):
    nm = cvec.shape[0]
    out = pl.pallas_call(
        _mod_kernel,
        grid=(DEPTH, N_MOD),
        in_specs=[
            pl.BlockSpec((nm, D_MODEL), lambda l, j: (0, 0)),
            pl.BlockSpec((1, D_MODEL, D_MODEL), lambda l, j: (l, 0, j)),
            pl.BlockSpec((1, 1, D_MODEL), lambda l, j: (l, 0, j)),
        ],
        out_specs=pl.BlockSpec((1, 1, nm, D_MODEL), lambda l, j: (l, j, 0, 0)),
        out_shape=jax.ShapeDtypeStruct((DEPTH, N_MOD, nm, D_MODEL), F32),
        compiler_params=_cparams(("parallel", "parallel")),
        name="mod_table",
    )(cvec, w_ada, b_ada.reshape(DEPTH, 1, N_MOD * D_MODEL))
    return jnp.transpose(out, (0, 2, 1, 3))


def _ffn_kernel(*refs, slot, gslot, n_first):
    if n_first is None:
        x_ref, mod_ref, ng_ref, w1_ref, w2_ref, o_ref = refs
        x = x_ref[...]
    else:
        x_ref, x2_ref, mod_ref, ng_ref, w1_ref, w2_ref, o_ref = refs
        x = jnp.where(pl.program_id(0) < n_first, x_ref[...], x2_ref[...])
    m = mod_ref[0]
    shift, scale, gate = m[slot:slot + 1], m[slot + 1:slot + 2], m[slot + 2:slot + 3]
    h = _rms(x) * ng_ref[gslot:gslot + 1, :]
    hb = (h * (1.0 + scale) + shift).astype(BF16)
    acc = None
    for c in range(D_FF // FF_CHUNK):
        lo = c * FF_CHUNK
        g = jnp.dot(hb, w1_ref[:, lo:lo + FF_CHUNK], preferred_element_type=F32)
        u = jnp.dot(hb, w1_ref[:, D_FF + lo:D_FF + lo + FF_CHUNK], preferred_element_type=F32)
        a = (_silu(g) * u).astype(BF16)
        part = jnp.dot(a, w2_ref[lo:lo + FF_CHUNK, :], preferred_element_type=F32)
        acc = part if acc is None else acc + part
    y = _rms(acc) * ng_ref[gslot + 1:gslot + 2, :]
    o_ref[...] = x + 0.5 * gate * y


def _half_ffn(s, mod, ng, w1, w2, *, slot, gslot, tm, n_tiles, tiles_per_sample, n_samples,
              tail=None):
    n_first = None if tail is None else s.shape[0] // tm
    rows_in = [pl.BlockSpec((tm, D_MODEL), lambda i: (i, 0))]
    args = [s]
    if tail is not None:
        rows_in = [pl.BlockSpec((tm, D_MODEL), lambda i: (jnp.minimum(i, n_first - 1), 0)),
                   pl.BlockSpec((tm, D_MODEL), lambda i: (jnp.maximum(i - n_first, 0), 0))]
        args = [s, tail]
    return pl.pallas_call(
        functools.partial(_ffn_kernel, slot=slot, gslot=gslot, n_first=n_first),
        grid=(n_tiles,),
        in_specs=rows_in + [
            pl.BlockSpec((1, N_MOD, D_MODEL),
                         lambda i: (jnp.minimum(i // tiles_per_sample, n_samples), 0, 0)),
            _const_spec(ng.shape),
            _const_spec(w1.shape),
            _const_spec(w2.shape),
        ],
        out_specs=pl.BlockSpec((tm, D_MODEL), lambda i: (i, 0)),
        out_shape=jax.ShapeDtypeStruct((n_tiles * tm, D_MODEL), F32),
        compiler_params=_cparams(("parallel",)),
        name="half_ffn",
    )(*args, mod, ng, w1, w2)


def _inproj_kernel(x_ref, mod_ref, ng_ref, w_ref, of_ref, oh_ref):
    m = mod_ref[0]
    h = _rms(x_ref[...]) * ng_ref[2:3, :]
    hb = (h * (1.0 + m[4:5]) + m[3:4]).astype(BF16)
    for lo in range(0, F_W, PROJ_TN):
        cs = slice(lo, min(lo + PROJ_TN, F_W))
        of_ref[:, cs] = jnp.dot(hb, w_ref[:, cs], preferred_element_type=F32)
    for j in range(H_W // PROJ_TN):
        cs = slice(j * PROJ_TN, (j + 1) * PROJ_TN)
        ws = slice(F_W + j * PROJ_TN, F_W + (j + 1) * PROJ_TN)
        oh_ref[:, cs] = jnp.dot(hb, w_ref[:, ws], preferred_element_type=F32).astype(BF16)


def _in_proj(s, mod, ng, w, *, tm, n_tiles, tiles_per_sample, n_samples):
    rows = n_tiles * tm
    return pl.pallas_call(
        _inproj_kernel,
        grid=(n_tiles,),
        in_specs=[
            pl.BlockSpec((tm, D_MODEL), lambda i: (i, 0)),
            pl.BlockSpec((1, N_MOD, D_MODEL),
                         lambda i: (jnp.minimum(i // tiles_per_sample, n_samples), 0, 0)),
            _const_spec(ng.shape),
            _const_spec(w.shape),
        ],
        out_specs=[
            pl.BlockSpec((tm, F_W), lambda i: (i, 0)),
            pl.BlockSpec((tm, H_W), lambda i: (i, 0)),
        ],
        out_shape=[jax.ShapeDtypeStruct((rows, F_W), F32),
                   jax.ShapeDtypeStruct((rows, H_W), BF16)],
        compiler_params=_cparams(("parallel",)),
        name="in_proj",
    )(s, mod, ng, w)


def _prep_kernel(x_ref, prev_ref, next_ref, small_ref, cw_ref, cb_ref, alog_ref, dtb_ref,
                 cvg_ref, cvs_ref, gb_ref, dt_ref, gt_ref, at_ref, *, nx_tiles,
                 x_tiles_per_seq, c_tiles_per_seq):
    i = pl.program_id(0)
    t = x_ref.shape[0]
    pos = jnp.where(i < nx_tiles, i % x_tiles_per_seq, (i - nx_tiles) % c_tiles_per_seq)
    last = jnp.where(i < nx_tiles, x_tiles_per_seq - 1, c_tiles_per_seq - 1)
    has_prev = (pos != 0).astype(F32)
    has_next = (pos != last).astype(F32)
    pad = CONV_K // 2

    for cc in range(CONV_W // 128):
        cs = slice(cc * 128, (cc + 1) * 128)
        ext = jnp.concatenate([prev_ref[:, cs] * has_prev, x_ref[:, cs],
                               next_ref[:, cs] * has_next], axis=0)
        y = cb_ref[:, cs] + cw_ref[pad:pad + 1, cs] * ext[8:8 + t]
        for k in range(CONV_K):
            if k != pad:
                shifted = pltpu.roll(ext, (pad - k) % (t + 16), 0)[8:8 + t]
                y = y + cw_ref[k:k + 1, cs] * shifted
        y = _silu(y)
        if cc < 2 * GDN_HEADS:
            y = y * lax.rsqrt(jnp.sum(y * y, axis=-1, keepdims=True) + EPS)
            if cc < GDN_HEADS:
                y = y * (GDN_DK ** -0.5)
        if cc < GDN_QKV // 128:
            cvg_ref[:, cs] = y
        else:
            cvs_ref[:, cc * 128 - GDN_QKV:(cc + 1) * 128 - GDN_QKV] = y

    sm = small_ref[...]
    lane = lax.broadcasted_iota(jnp.int32, (t, 128), 1)
    row = lax.broadcasted_iota(jnp.int32, (t, 128), 0)
    z = sm + dtb_ref[...]
    sp = jnp.maximum(z, 0.0) + jnp.log1p(jnp.exp(-jnp.abs(z)))
    g = -jnp.exp(alog_ref[...]) * sp
    is_gdn = lane < LANE_GB
    is_ssd = (lane >= LANE_DT) & (lane < LANE_DT + 2 * SSD_HEADS)
    win = jnp.where(is_gdn, GDN_CHUNK, SSD_CHUNK)
    rmod = row & (win - 1)
    g = jnp.where(is_gdn | is_ssd, g, 0.0)
    pre, suf = g, g
    sft = 1
    while sft < SSD_CHUNK:
        pre = pre + jnp.where(rmod >= sft, pltpu.roll(pre, sft, 0), 0.0)
        suf = suf + jnp.where(rmod < win - sft, pltpu.roll(suf, t - sft, 0), 0.0)
        sft *= 2
    backward = (is_gdn & (lane >= GDN_HEADS)) | (is_ssd & (lane >= LANE_DT + SSD_HEADS))
    cum = jnp.where(backward, suf, pre)
    is_beta = (lane >= LANE_GB) & (lane < LANE_DT)
    gb = jnp.where(is_beta, jax.nn.sigmoid(sm), cum)
    gb_ref[...] = gb
    dt_ref[...] = sp
    gbt = gb.T
    for c in range(t // GDN_CHUNK):
        gt_ref[c] = gbt[0:16, c * GDN_CHUNK:(c + 1) * GDN_CHUNK]
    for c in range(t // SSD_CHUNK):
        at_ref[c] = gbt[16:32, c * SSD_CHUNK:(c + 1) * SSD_CHUNK]


def _prep(f, cw, cb, alog, dtb, *, n_tiles, nx_tiles, x_tiles_per_seq, c_tiles_per_seq):
    t = SEQ_T
    rows = n_tiles * t
    r8 = rows // 8
    t8 = t // 8
    return pl.pallas_call(
        functools.partial(_prep_kernel, nx_tiles=nx_tiles, x_tiles_per_seq=x_tiles_per_seq,
                          c_tiles_per_seq=c_tiles_per_seq),
        grid=(n_tiles,),
        in_specs=[
            pl.BlockSpec((t, CONV_W), lambda i: (i, 0)),
            pl.BlockSpec((8, CONV_W), lambda i: (jnp.maximum(i * t8 - 1, 0), 0)),
            pl.BlockSpec((8, CONV_W), lambda i: (jnp.minimum((i + 1) * t8, r8 - 1), 0)),
            pl.BlockSpec((t, 128), lambda i: (i, F_SMALL // 128)),
            _const_spec(cw.shape),
            _const_spec(cb.shape),
            _const_spec(alog.shape),
            _const_spec(dtb.shape),
        ],
        out_specs=[
            pl.BlockSpec((t, GDN_QKV), lambda i: (i, 0)),
            pl.BlockSpec((t, SSD_XBC), lambda i: (i, 0)),
            pl.BlockSpec((t, 128), lambda i: (i, 0)),
            pl.BlockSpec((t, 128), lambda i: (i, 0)),
            pl.BlockSpec((t // GDN_CHUNK, 16, GDN_CHUNK), lambda i: (i, 0, 0)),
            pl.BlockSpec((t // SSD_CHUNK, 16, SSD_CHUNK), lambda i: (i, 0, 0)),
        ],
        out_shape=[
            jax.ShapeDtypeStruct((rows, GDN_QKV), F32),
            jax.ShapeDtypeStruct((rows, SSD_XBC), F32),
            jax.ShapeDtypeStruct((rows, 128), F32),
            jax.ShapeDtypeStruct((rows, 128), F32),
            jax.ShapeDtypeStruct((rows // GDN_CHUNK, 16, GDN_CHUNK), F32),
            jax.ShapeDtypeStruct((rows // SSD_CHUNK, 16, SSD_CHUNK), F32),
        ],
        compiler_params=_cparams(("parallel",)),
        name="conv_prep",
    )(f, f, f, f, cw, cb, alog, dtb)


def _rope_kernel(xq_ref, xk_ref, v_ref, cos_ref, sin_ref, q_ref, kv_ref, *, nx_tiles):
    i = pl.program_id(0)
    is_x = i < nx_tiles
    cos, sin = cos_ref[...], sin_ref[...]
    for x_ref, o_ref, scale in ((xq_ref, q_ref, Q_SCALE), (xk_ref, kv_ref, None)):
        for b in range(DIFF_HEADS):
            bs = slice(b * 128, (b + 1) * 128)
            v = x_ref[:, bs]
            r = v * cos + pltpu.roll(v, 64, 1) * sin
            r = jnp.where(is_x, r, v)
            o_ref[:, bs] = (r if scale is None else r * scale).astype(BF16)
    kv_ref[:, 512:] = v_ref[...]


def _rope(f, hh, cos, sin, *, n_tiles, nx_tiles, x_tiles_per_seq, c_tiles_per_seq):
    t = SEQ_T
    per_sample = x_tiles_per_seq + c_tiles_per_seq
    tab = pl.BlockSpec((t, 128), lambda i: (jnp.where(i < nx_tiles, i % x_tiles_per_seq, 0), 0))

    def kv_block(i):
        j = i - nx_tiles
        return jnp.where(i < nx_tiles,
                         (i // x_tiles_per_seq) * per_sample + i % x_tiles_per_seq,
                         (j // c_tiles_per_seq) * per_sample + x_tiles_per_seq + j % c_tiles_per_seq)

    return pl.pallas_call(
        functools.partial(_rope_kernel, nx_tiles=nx_tiles),
        grid=(n_tiles,),
        in_specs=[pl.BlockSpec((t, 512), lambda i: (i, F_DQ // 512)),
                  pl.BlockSpec((t, 512), lambda i: (i, F_DQ // 512 + 1)),
                  pl.BlockSpec((t, 512), lambda i: (i, H_DV // 512)), tab, tab],
        out_specs=[pl.BlockSpec((t, 512), lambda i: (i, 0)),
                   pl.BlockSpec((t, 1024), lambda i: (kv_block(i), 0))],
        out_shape=[jax.ShapeDtypeStruct((n_tiles * t, 512), BF16),
                   jax.ShapeDtypeStruct((n_tiles * t, 1024), BF16)],
        compiler_params=_cparams(("parallel",)),
        name="rope",
    )(f, f, hh, cos, sin)


def _when(cond, fn):
    if isinstance(cond, bool):
        if cond:
            fn()
    else:
        pl.when(cond)(fn)


def _attn_kernel(lam_ref, nw_ref, q_ref, k_ref, v_ref, *rest, tq, tk, n_q, n_k, lam_init):
    o_ref, s0, s1, t0, t1, acc_scr, m_scr = rest[-7:]
    s_scr, mt_scr = (s0, s1), (t0, t1)
    half0 = (lax.broadcasted_iota(jnp.int32, (tq, 128), 1) & (DIFF_DQK // 2)) == 0
    ones_col = (lax.broadcasted_iota(jnp.int32, (tk, 128), 1) == 0).astype(BF16)
    n_steps = n_q * n_k

    def rows_of(i, size):
        r0 = i * size
        if not isinstance(i, int):
            r0 = pl.multiple_of(r0, size)
        return pl.ds(r0, size)

    def split(n):
        return (n // n_k, n % n_k)

    def scores(n, slot):
        qi, t = split(n)
        q = q_ref[rows_of(qi, tq), :]
        zero = jnp.zeros_like(q)
        k = k_ref[rows_of(t, tk), :]
        for j, keep in enumerate((half0, jnp.logical_not(half0))):
            s = lax.dot_general(jnp.where(keep, q, zero), k, (((1,), (1,)), ((), ())),
                                preferred_element_type=F32)
            s_scr[slot][j] = s
            mt_scr[slot][j] = jnp.broadcast_to(jnp.max(s, axis=1, keepdims=True), (tq, 128))

    def consume(n, slot):
        _, t = split(n)
        v = jnp.concatenate([v_ref[rows_of(t, tk), :], ones_col], axis=1)
        m_prevs = [m_scr[0], m_scr[1]]
        accs = [acc_scr[0], acc_scr[1]]
        for j in range(2):
            m_new = jnp.maximum(m_prevs[j], mt_scr[slot][j])
            m_scr[j] = m_new
            alpha = jnp.exp2(m_prevs[j] - m_new)
            p = jnp.exp2((s_scr[slot][j]
                          - jnp.concatenate([m_new] * (tk // 128), axis=1)).astype(BF16))
            acc_scr[j] = (jnp.concatenate([alpha, alpha], axis=1) * accs[j]
                          + jnp.dot(p, v, preferred_element_type=F32))

    def finalize(qi):
        lam = lam_ref[...]
        lam_full = (jnp.exp(jnp.sum(lam[0:1] * lam[1:2], axis=-1, keepdims=True))
                    - jnp.exp(jnp.sum(lam[2:3] * lam[3:4], axis=-1, keepdims=True)) + lam_init)
        a0, a1 = acc_scr[0], acc_scr[1]
        o = (a0[:, :DIFF_DV] / a0[:, DIFF_DV:DIFF_DV + 1]
             - lam_full * (a1[:, :DIFF_DV] / a1[:, DIFF_DV:DIFF_DV + 1]))
        o_ref[rows_of(qi, tq), :] = (_rms(o) * nw_ref[...] * (1.0 - lam_init)).astype(BF16)

    def reset_max():
        m_scr[...] = jnp.full(m_scr.shape, -jnp.inf, F32)

    def region(n, slot, last=False):
        qi, t = split(n)
        _when(t == 0, reset_max)
        if not last:
            scores(n + 1, 1 - slot)
        consume(n, slot)
        _when(t == n_k - 1, lambda: finalize(qi))

    acc_scr[...] = jnp.zeros_like(acc_scr)
    scores(0, 0)
    n_pair = (n_steps - 1) // 2

    def body(u, carry):
        region(2 * u, 0)
        region(2 * u + 1, 1)
        return carry

    if n_pair > 0:
        lax.fori_loop(0, n_pair, body, 0)
    for n in range(2 * n_pair, n_steps):
        region(n, n % 2, last=n == n_steps - 1)


def _attn_scratch(tq, tk):
    return ([pltpu.VMEM((2, tq, tk), F32)] * 2 + [pltpu.VMEM((2, tq, 128), F32)] * 2
            + [pltpu.VMEM((2, tq, 2 * DIFF_DV), F32), pltpu.VMEM((2, tq, 128), F32)])


def _attention(q, kv, lam, nw, *, bsz, seq, ctx_len, lam_init, with_ctx):
    tq = 1024 if seq % 1024 == 0 else 512
    n_keys = seq + ctx_len
    tk = 768 if n_keys % 768 == 0 else ctx_len
    rows = bsz * n_keys
    out = pl.pallas_call(
        functools.partial(_attn_kernel, tq=tq, tk=tk, n_q=seq // tq, n_k=n_keys // tk,
                          lam_init=lam_init),
        grid=(bsz, DIFF_HEADS),
        in_specs=[
            _const_spec(lam.shape),
            _const_spec(nw.shape),
            pl.BlockSpec((seq, 128), lambda b, h: (b, h)),
            pl.BlockSpec((n_keys, 128), lambda b, h: (b, h)),
            pl.BlockSpec((n_keys, 128), lambda b, h: (b, DIFF_HEADS + h)),
        ],
        out_specs=pl.BlockSpec((seq, 128), lambda b, h: (b, h)),
        out_shape=jax.ShapeDtypeStruct((rows, DIFF_HEADS * DIFF_DV), BF16),
        scratch_shapes=_attn_scratch(tq, tk),
        compiler_params=_cparams(("parallel", "parallel")),
        name="diff_attn",
    )(lam, nw, q, kv, kv)
    if not with_ctx:
        return out
    qblk = bsz * seq // ctx_len
    per_sample = n_keys // ctx_len

    def ctx_keys(col):
        return lambda b, h: (b * per_sample + per_sample - 1, col + h)

    return pl.pallas_call(
        functools.partial(_attn_kernel, tq=ctx_len, tk=ctx_len, n_q=1, n_k=1, lam_init=lam_init),
        grid=(bsz, DIFF_HEADS),
        in_specs=[
            _const_spec(lam.shape),
            _const_spec(nw.shape),
            pl.BlockSpec((ctx_len, 128), lambda b, h: (qblk + b, h)),
            pl.BlockSpec((ctx_len, 128), ctx_keys(0)),
            pl.BlockSpec((ctx_len, 128), ctx_keys(DIFF_HEADS)),
            pl.BlockSpec(memory_space=pl.ANY),
        ],
        out_specs=pl.BlockSpec((ctx_len, 128), lambda b, h: (qblk + b, h)),
        out_shape=jax.ShapeDtypeStruct((rows, DIFF_HEADS * DIFF_DV), BF16),
        scratch_shapes=_attn_scratch(ctx_len, ctx_len),
        input_output_aliases={5: 0},
        compiler_params=_cparams(("parallel", "parallel")),
        name="diff_attn_ctx",
    )(lam, nw, q, kv, kv, out)


def _seq_maps(bsz, nsx, nsc):
    cbase = bsz * nsx

    def fwd(b, s):
        return jnp.where(s < nsc, cbase + b * nsc + s, b * nsx + (s - nsc))

    def bwd(b, s):
        return jnp.where(s < nsc, cbase + b * nsc + (nsc - 1 - s), b * nsx + (nsx - 1 - (s - nsc)))

    return fwd, bwd


def _gdn_kernel(cvf_ref, gbf_ref, gtf_ref, cvb_ref, gbb_ref, gtb_ref, of_ref, ob_ref, st_ref):
    @pl.when(pl.program_id(1) == 0)
    def _():
        st_ref[...] = jnp.zeros_like(st_ref)

    c_ = GDN_CHUNK
    nck = cvf_ref.shape[0] // c_
    ii = lax.broadcasted_iota(jnp.int32, (c_, c_), 0)
    jj = lax.broadcasted_iota(jnp.int32, (c_, c_), 1)
    eye = (ii == jj).astype(F32)
    blk16 = (ii // 16) == (jj // 16)
    same32 = (ii // 32) == (jj // 32)
    lvl32 = same32 & jnp.logical_not(blk16)
    lvl64 = jnp.logical_not(same32)
    dirs = (
        (cvf_ref, gbf_ref, gtf_ref, of_ref, ii >= jj, ii > jj, c_ - 1),
        (cvb_ref, gbb_ref, gtb_ref, ob_ref, ii <= jj, ii < jj, 0),
    )

    class _P:
        pass

    probs = []
    for d, (cv_ref, gb_ref, gt_ref, o_ref, incl, strict, last_row) in enumerate(dirs):
        for c in range(nck):
            cc = c if d == 0 else nck - 1 - c
            for h in range(GDN_HEADS):
                p = _P()
                p.d, p.h, p.c, p.cc = d, h, c, cc
                p.cv, p.gb, p.gt, p.o = cv_ref, gb_ref, gt_ref, o_ref
                p.incl, p.strict, p.last_row = incl, strict, last_row
                p.rows = slice(cc * c_, (cc + 1) * c_)
                p.idx = d * GDN_HEADS + h
                probs.append(p)

    def q_of(p):
        return p.cv[p.rows, p.h * GDN_DK:(p.h + 1) * GDN_DK]

    def k_of(p):
        return p.cv[p.rows, 512 + p.h * GDN_DK:512 + (p.h + 1) * GDN_DK]

    def v_of(p):
        return p.cv[p.rows, 1024 + p.h * GDN_DV:1024 + (p.h + 1) * GDN_DV]

    def gcol_of(p):
        return p.gb[p.rows, :][:, LANE_GA + p.idx:LANE_GA + p.idx + 1]

    def beta_of(p):
        return p.gb[p.rows, :][:, LANE_GB + p.idx:LANE_GB + p.idx + 1]

    def glast_of(p):
        r = p.last_row
        return p.gb[p.rows, :][r:r + 1, LANE_GA + p.idx:LANE_GA + p.idx + 1]

    for p in probs:
        kh = k_of(p)
        grow = p.gt[p.cc][p.idx:p.idx + 1, :]
        dec = jnp.where(p.incl, jnp.exp(jnp.minimum(gcol_of(p) - grow, 0.0)), 0.0)
        ga = _bdot_nt(jnp.concatenate([kh * beta_of(p), q_of(p)], axis=0), kh)
        p.m = jnp.where(p.strict, ga[:c_] * dec, 0.0)
        p.aqk = ga[c_:] * dec
    for p in probs:
        mb = jnp.where(blk16, p.m, 0.0)
        p.tinv = eye - mb
        p.sq = _bdot(mb, mb)
    for step in range(3):
        for p in probs:
            p.tinv = p.tinv + _bdot(p.tinv, p.sq)
            if step < 2:
                p.sq = _bdot(p.sq, p.sq)
    for lvl in (lvl32, lvl64):
        for p in probs:
            p.x = _bdot(jnp.where(lvl, p.m, 0.0), p.tinv)
        for p in probs:
            p.tinv = p.tinv - _bdot(p.tinv, p.x)
    for p in probs:
        beta = beta_of(p)
        rhs = jnp.concatenate([v_of(p) * beta, k_of(p) * (beta * jnp.exp(gcol_of(p)))], axis=1)
        p.uw = _bdot(p.tinv, rhs)
    states = [[st_ref[d, h] for h in range(GDN_HEADS)] for d in range(2)]
    for c in range(nck):
        cur = [p for p in probs if p.c == c]
        for p in cur:
            lhs = jnp.concatenate([p.uw[:, GDN_DV:], q_of(p) * jnp.exp(gcol_of(p))], axis=0)
            p.wq = _bdot(lhs, states[p.d][p.h])
        for p in cur:
            p.v_new = p.uw[:, :GDN_DV] - p.wq[:c_]
            glast = glast_of(p)
            k_dec = k_of(p) * jnp.exp(glast - gcol_of(p))
            states[p.d][p.h] = (states[p.d][p.h] * jnp.exp(glast)
                                + _bdot_tn(k_dec, p.v_new))
        for p in cur:
            p.out = p.wq[c_:] + _bdot(p.aqk, p.v_new)
    for p in probs:
        p.o[p.rows, p.h * GDN_DV:(p.h + 1) * GDN_DV] = p.out
    for d in range(2):
        for h in range(GDN_HEADS):
            st_ref[d, h] = states[d][h]


def _gdn(cv, gb, gt, *, bsz, nsx, nsc):
    t = SEQ_T
    rows = cv.shape[0]
    fwd, bwd = _seq_maps(bsz, nsx, nsc)
    nck = t // GDN_CHUNK

    def specs(fn):
        return [
            pl.BlockSpec((t, GDN_QKV), lambda b, s: (fn(b, s), 0)),
            pl.BlockSpec((t, 128), lambda b, s: (fn(b, s), 0)),
            pl.BlockSpec((nck, 16, GDN_CHUNK), lambda b, s: (fn(b, s), 0, 0)),
        ]

    return pl.pallas_call(
        _gdn_kernel,
        grid=(bsz, nsc + nsx),
        in_specs=specs(fwd) + specs(bwd),
        out_specs=[pl.BlockSpec((t, 512), lambda b, s: (fwd(b, s), 0)),
                   pl.BlockSpec((t, 512), lambda b, s: (bwd(b, s), 0))],
        out_shape=[jax.ShapeDtypeStruct((rows, 512), F32)] * 2,
        scratch_shapes=[pltpu.VMEM((2, GDN_HEADS, GDN_DK, GDN_DV), F32)],
        compiler_params=_cparams(("parallel", "arbitrary")),
        name="gdn_scan",
    )(cv, gb, gt, cv, gb, gt)


def _split_dot(x, sel):
    hi = x.astype(BF16)
    lo = (x - hi.astype(F32)).astype(BF16)
    return (jnp.dot(hi, sel, preferred_element_type=F32)
            + jnp.dot(lo, sel, preferred_element_type=F32))


def _ssd_kernel(dsk_ref, sel64_ref, cvf_ref, gbf_ref, dtf_ref, atf_ref, cvb_ref,
                gbb_ref, dtb_ref, atb_ref, yf_ref, yb_ref, st_ref):
    @pl.when(pl.program_id(1) == 0)
    def _():
        st_ref[...] = jnp.zeros_like(st_ref)

    c_ = SSD_CHUNK
    p_ = SSD_HEADDIM
    hpg = SSD_HEADS // SSD_GROUPS
    gw = hpg * p_
    nck = cvf_ref.shape[0] // c_
    ii = lax.broadcasted_iota(jnp.int32, (c_, c_), 0)
    jj = lax.broadcasted_iota(jnp.int32, (c_, c_), 1)
    first_half = lax.broadcasted_iota(jnp.int32, (c_, 2 * p_), 1) < p_
    dirs = (
        (cvf_ref, gbf_ref, dtf_ref, atf_ref, yf_ref, ii >= jj, c_ - 1),
        (cvb_ref, gbb_ref, dtb_ref, atb_ref, yb_ref, ii <= jj, 0),
    )

    class _P:
        pass

    probs = []
    for d, (cv_ref, gb_ref, dt_ref, at_ref, y_ref, incl, last_row) in enumerate(dirs):
        for c in range(nck):
            p = _P()
            p.d, p.c = d, c
            p.cc = c if d == 0 else nck - 1 - c
            p.rows = slice(p.cc * c_, (p.cc + 1) * c_)
            p.cv, p.y, p.incl, p.last_row = cv_ref, y_ref, incl, last_row
            p.at = at_ref[p.cc]
            p.acs, dts = gb_ref[p.rows, :], dt_ref[p.rows, :]
            p.a64 = _split_dot(p.acs, sel64_ref[d])
            xs = cv_ref[p.rows, 0:SSD_INNER]
            p.xdt = xs * _split_dot(dts, sel64_ref[d])
            alast = p.a64[last_row:last_row + 1, :]
            p.xd = p.xdt * jnp.exp(alast - p.a64)
            p.sdec = jnp.exp(alast)
            p.skip = dsk_ref[...] * xs if d == 0 else None
            probs.append(p)

    def b_of(p, g):
        return p.cv[p.rows, SSD_INNER + g * SSD_STATE:SSD_INNER + (g + 1) * SSD_STATE]

    def c_of(p, g):
        return p.cv[p.rows, SSD_INNER + (SSD_GROUPS + g) * SSD_STATE:
                    SSD_INNER + (SSD_GROUPS + g + 1) * SSD_STATE]

    states = [[st_ref[d, g] for g in range(SSD_GROUPS)] for d in range(2)]

    def chain_read(c):
        for p in probs:
            if p.c == c:
                p.y_off = [_bdot(c_of(p, g), states[p.d][g]) for g in range(SSD_GROUPS)]

    def chain_write(c):
        for p in probs:
            if p.c == c:
                for g in range(SSD_GROUPS):
                    gs = slice(g * gw, (g + 1) * gw)
                    states[p.d][g] = (states[p.d][g] * p.sdec[:, gs]
                                      + _bdot_tn(b_of(p, g), p.xd[:, gs]))

    def diag(c):
        for p in probs:
            if p.c != c:
                continue
            p.y_diag = []
            for g in range(SSD_GROUPS):
                cb = p.cb[g]
                for pair in range(hpg // 2):
                    lo_h = g * hpg + 2 * pair
                    xpair = p.xdt[:, lo_h * p_:(lo_h + 2) * p_]
                    halves = []
                    for h in (lo_h, lo_h + 1):
                        arow = p.at[p.d * SSD_HEADS + h:p.d * SSD_HEADS + h + 1, :]
                        ln = LANE_DT + p.d * SSD_HEADS + h
                        acol = p.acs[:, ln:ln + 1]
                        ldec = jnp.where(p.incl, jnp.exp(jnp.minimum(acol - arow, 0.0)), 0.0)
                        halves.append(_bdot(cb * ldec, xpair))
                    p.y_diag.append(jnp.where(first_half, halves[0], halves[1]))

    chain_read(0)
    for p in probs:
        p.cb = [_bdot_nt(c_of(p, g), b_of(p, g)) for g in range(SSD_GROUPS)]
    chain_write(0)
    diag(0)
    for c in range(1, nck):
        chain_read(c)
        diag(c)
        chain_write(c)
    for p in probs:
        y = jnp.concatenate(p.y_diag, axis=1) + jnp.concatenate(p.y_off, axis=1) * jnp.exp(p.a64)
        if p.skip is not None:
            y = y + p.skip
        p.y[p.rows, :] = y
    for d in range(2):
        for g in range(SSD_GROUPS):
            st_ref[d, g] = states[d][g]


def _ssd(dsk, sel64, cv, gb, dt, at, *, bsz, nsx, nsc):
    t = SEQ_T
    rows = cv.shape[0]
    fwd, bwd = _seq_maps(bsz, nsx, nsc)
    nck = t // SSD_CHUNK

    def specs(fn):
        return [
            pl.BlockSpec((t, SSD_XBC), lambda b, s: (fn(b, s), 0)),
            pl.BlockSpec((t, 128), lambda b, s: (fn(b, s), 0)),
            pl.BlockSpec((t, 128), lambda b, s: (fn(b, s), 0)),
            pl.BlockSpec((nck, 16, SSD_CHUNK), lambda b, s: (fn(b, s), 0, 0)),
        ]

    return pl.pallas_call(
        _ssd_kernel,
        grid=(bsz, nsc + nsx),
        in_specs=([_const_spec(dsk.shape), _const_spec(sel64.shape)]
                  + specs(fwd) + specs(bwd)),
        out_specs=[pl.BlockSpec((t, SSD_INNER), lambda b, s: (fwd(b, s), 0)),
                   pl.BlockSpec((t, SSD_INNER), lambda b, s: (bwd(b, s), 0))],
        out_shape=[jax.ShapeDtypeStruct((rows, SSD_INNER), F32)] * 2,
        scratch_shapes=[pltpu.VMEM((2, SSD_GROUPS, SSD_STATE, SSD_INNER // SSD_GROUPS), F32)],
        compiler_params=_cparams(("parallel", "arbitrary")),
        name="ssd_scan",
    )(dsk, sel64, cv, gb, dt, at, cv, gb, dt, at)


def _merge_kernel(x_ref, mod_ref, ng_ref, gof_ref, gob_ref, gz_ref, att_ref, syf_ref, syb_ref,
                  sz_ref, gate_ref, gnw_ref, snw_ref, wb_ref, wo_ref, o_ref):
    oa = gof_ref[...] + gob_ref[...]
    parts = []
    for h in range(GDN_HEADS):
        hs = slice(h * GDN_DV, (h + 1) * GDN_DV)
        parts.append(_rms(oa[:, hs]) * gnw_ref[:, hs])
    oa = jnp.concatenate(parts, axis=1) * _silu(gz_ref[...].astype(F32))
    oc = (syf_ref[...] + syb_ref[...]) * _silu(sz_ref[...].astype(F32))
    oc = _rms(oc) * snw_ref[...]
    outs = (oa.astype(BF16), att_ref[...], oc.astype(BF16))
    y = None
    for n in range(N_BRANCH):
        gate = jax.nn.sigmoid(gate_ref[:, n * D_MODEL:(n + 1) * D_MODEL].astype(F32))
        term = gate * jnp.dot(outs[n], wb_ref[n], preferred_element_type=F32)
        y = term if y is None else y + term
    y = jnp.dot(y.astype(BF16), wo_ref[...], preferred_element_type=F32)
    m = mod_ref[0]
    o_ref[...] = x_ref[...] + m[5:6] * (_rms(y) * ng_ref[3:4, :])


def _merge(s, mod, ng, gof, gob, hh, att, syf, syb, gnw, snw, wb, wo, *, tm, n_tiles,
           tiles_per_sample, n_samples):
    rows = n_tiles * tm
    row512 = pl.BlockSpec((tm, 512), lambda i: (i, 0))
    return pl.pallas_call(
        _merge_kernel,
        grid=(n_tiles,),
        in_specs=[
            pl.BlockSpec((tm, D_MODEL), lambda i: (i, 0)),
            pl.BlockSpec((1, N_MOD, D_MODEL),
                         lambda i: (jnp.minimum(i // tiles_per_sample, n_samples), 0, 0)),
            _const_spec(ng.shape),
            row512, row512,
            pl.BlockSpec((tm, 512), lambda i: (i, H_GZ // 512)),
            row512, row512, row512,
            pl.BlockSpec((tm, 512), lambda i: (i, H_SZ // 512)),
            pl.BlockSpec((tm, N_BRANCH * D_MODEL), lambda i: (i, 0)),
            _const_spec(gnw.shape),
            _const_spec(snw.shape),
            _const_spec(wb.shape),
            _const_spec(wo.shape),
        ],
        out_specs=pl.BlockSpec((tm, D_MODEL), lambda i: (i, 0)),
        out_shape=jax.ShapeDtypeStruct((rows, D_MODEL), F32),
        compiler_params=_cparams(("parallel",)),
        name="merge",
    )(s, mod, ng, gof, gob, hh, att, syf, syb, hh, hh, gnw, snw, wb, wo)


def _deinterleave_pairs(w):
    d = w.shape[0]
    w = w.reshape(d, DIFF_HEADS, 2, DIFF_DQK // 2, 2)
    return jnp.transpose(w, (0, 1, 4, 2, 3)).reshape(d, DIFF_HEADS * 2 * DIFF_DQK)


def _proj_weight(w_in):
    w_in = w_in.astype(BF16)
    o = 0
    parts = {}
    for name, width in (("gqkv", GDN_QKV), ("gz", 512), ("ga", 8), ("gb", 8), ("dq", 512),
                        ("dk", 512), ("dv", 512), ("sz", 512), ("sxbc", SSD_XBC), ("sdt", 16),
                        ("gate", N_BRANCH * D_MODEL)):
        parts[name] = w_in[:, o:o + width]
        o += width
    small_pad = jnp.zeros((w_in.shape[0], SMALL_W - 32), w_in.dtype)
    cols = [parts["gqkv"], parts["sxbc"],
            _deinterleave_pairs(parts["dq"]), _deinterleave_pairs(parts["dk"]),
            parts["ga"], parts["gb"], parts["sdt"], small_pad,
            parts["gate"], parts["dv"], parts["gz"], parts["sz"]]
    return jnp.concatenate(cols, axis=1)


def _rope_tables(seq):
    rows = seq // GRID_W
    row = jnp.repeat(jnp.arange(rows, dtype=F32), GRID_W)
    col = jnp.tile(jnp.arange(GRID_W, dtype=F32), rows)
    inv = ROPE_BASE ** (-jnp.arange(ROPE_PAIRS_PER_AXIS, dtype=F32) / ROPE_PAIRS_PER_AXIS)
    ang = jnp.concatenate([row[:, None] * inv, col[:, None] * inv], axis=-1)
    cos, sin = jnp.cos(ang), jnp.sin(ang)
    return jnp.tile(cos, (1, 4)), jnp.concatenate([-sin, -sin, sin, sin], axis=1)


def _ssd_selector():
    rows = jnp.arange(128)[None, :, None]
    cols = jnp.arange(SSD_INNER)[None, None, :]
    d = jnp.arange(2)[:, None, None]
    return (rows == LANE_DT + d * SSD_HEADS + cols // SSD_HEADDIM).astype(BF16)


def _lane_vec(pieces):
    v = jnp.zeros((128,), F32)
    for off, val in pieces:
        v = lax.dynamic_update_slice(v, val.reshape(-1).astype(F32), (off,))
    return v.reshape(1, 128)


def kernel(x, c, ctx, c_ctx, w_ada, b_ada, norm_g, w_ffn_in, w_ffn_out, w_in, gdn_conv_w, gdn_a_log,
           gdn_dt_bias, gdn_norm_w, diff_lambda, diff_norm_w, ssd_conv_w, ssd_conv_b, ssd_a_log,
           ssd_dt_bias, ssd_d, ssd_norm_w, w_branch, w_out):
    bsz, seq, _ = x.shape
    ctx_len = ctx.shape[1]
    t = SEQ_T
    tm = 512
    assert ctx_len == t and seq % 512 == 0 and (bsz * ctx_len) % tm == 0 and seq % GRID_W == 0
    nx_rows, nc_rows = bsz * seq, bsz * ctx_len
    n_rows = nx_rows + nc_rows

    nm = -(-(bsz + 1) // 8) * 8
    cvec = jnp.zeros((nm, D_MODEL), F32).at[:bsz].set(c).at[bsz].set(c_ctx)
    mod_all = _mod_table(cvec, w_ada, b_ada)
    cos, sin = _rope_tables(seq)
    sel64 = _ssd_selector()

    tok = dict(tm=tm, tiles_per_sample=seq // tm, n_samples=bsz)
    all_tiles, x_tiles = n_rows // tm, nx_rows // tm
    seq_kw = dict(n_tiles=n_rows // t, nx_tiles=nx_rows // t, x_tiles_per_seq=seq // t)
    nsx, nsc = seq // t, ctx_len // t

    s = None
    for i in range(DEPTH):
        last = i == DEPTH - 1
        lam_init = 0.8 - 0.6 * math.exp(-0.3 * i)
        mod, ng = mod_all[i], norm_g[i]
        w1 = [w_ffn_in[i, j].astype(BF16) for j in range(2)]
        w2 = [w_ffn_out[i, j].astype(BF16) for j in range(2)]

        tail = ctx.reshape(nc_rows, D_MODEL) if i == 0 else None
        s = _half_ffn(x.reshape(nx_rows, D_MODEL) if i == 0 else s, mod, ng, w1[0], w2[0], slot=0,
                      gslot=0, n_tiles=all_tiles, tail=tail, **tok)

        f, hh = _in_proj(s, mod, ng, _proj_weight(w_in[i]), n_tiles=all_tiles, **tok)
        cw = jnp.concatenate([gdn_conv_w[i], ssd_conv_w[i]], axis=1)
        cb = jnp.concatenate([jnp.zeros((GDN_QKV,), F32), ssd_conv_b[i]]).reshape(1, CONV_W)
        alog = _lane_vec([(LANE_GA, gdn_a_log[i]), (LANE_DT, ssd_a_log[i])])
        dtb = _lane_vec([(LANE_GA, gdn_dt_bias[i]), (LANE_DT, ssd_dt_bias[i])])
        cvg, cvs, gb, dt, gt, at = _prep(f, cw, cb, alog, dtb, c_tiles_per_seq=ctx_len // t, **seq_kw)
        qr, kv = _rope(f, hh, cos, sin, c_tiles_per_seq=ctx_len // t, **seq_kw)

        gof, gob = _gdn(cvg, gb, gt, bsz=bsz, nsx=nsx, nsc=nsc)
        dsk = jnp.repeat(ssd_d[i].astype(F32), SSD_HEADDIM).reshape(1, SSD_INNER)
        syf, syb = _ssd(dsk, sel64, cvs, gb, dt, at, bsz=bsz, nsx=nsx, nsc=nsc)
        att = _attention(qr, kv, diff_lambda[i], diff_norm_w[i].reshape(1, DIFF_DV), bsz=bsz,
                         seq=seq, ctx_len=ctx_len, lam_init=lam_init, with_ctx=not last)

        n_out = x_tiles if last else all_tiles
        gnw = jnp.tile(gdn_norm_w[i], GDN_HEADS).reshape(1, GDN_HEADS * GDN_DV)
        snw = ssd_norm_w[i].reshape(1, SSD_INNER)
        s = _merge(s, mod, ng, gof, gob, hh, att, syf, syb, gnw, snw, w_branch[i].astype(BF16),
                   w_out[i].astype(BF16), n_tiles=n_out, **tok)
        s = _half_ffn(s, mod, ng, w1[1], w2[1], slot=6, gslot=4, n_tiles=n_out, **tok)

    return s[:nx_rows].reshape(bsz, seq, D_MODEL)
```

```python
import functools
import math

import jax
import jax.numpy as jnp
from jax import lax
from jax.experimental import pallas as pl
from jax.experimental.pallas import tpu as pltpu

F32 = jnp.float32
BF16 = jnp.bfloat16

D_MODEL = 1024
DEPTH = 2
GRID_W = 64
EPS = 1e-6
N_MOD = 9
D_FF = 2816
CONV_K = 5
GDN_HEADS = 4
GDN_DK = 128
GDN_DV = 128
GDN_CHUNK = 64
DIFF_HEADS = 4
DIFF_DQK = 64
DIFF_DV = 128
ROPE_BASE = 10000.0
ROPE_PAIRS_PER_AXIS = DIFF_DQK // 4
SSD_HEADS = 8
SSD_HEADDIM = 64
SSD_INNER = SSD_HEADS * SSD_HEADDIM
SSD_GROUPS = 2
SSD_STATE = 128
SSD_CHUNK = 128
N_BRANCH = 3
BRANCH_W = 512

GDN_QKV = 2 * GDN_HEADS * GDN_DK + GDN_HEADS * GDN_DV
SSD_XBC = SSD_INNER + 2 * SSD_GROUPS * SSD_STATE
CONV_W = GDN_QKV + SSD_XBC
SMALL_W = 128
F_DQ = CONV_W
F_SMALL = F_DQ + 2 * 512
F_W = F_SMALL + SMALL_W
H_DV = N_BRANCH * D_MODEL
H_GZ = H_DV + 512
H_SZ = H_GZ + 512
H_W = H_SZ + 512
LANE_GA = 0
LANE_GB = 8
LANE_DT = 16

Q_SCALE = DIFF_DQK ** -0.5 * math.log2(math.e)

FF_CHUNK = 256
PROJ_TN = 512
SEQ_T = 256
VMEM_LIMIT = 56 * 1024 * 1024


def _cparams(sem):
    return pltpu.CompilerParams(dimension_semantics=sem, vmem_limit_bytes=VMEM_LIMIT)


def _const_spec(shape):
    nd = len(shape)
    return pl.BlockSpec(shape, lambda *_: (0,) * nd, pipeline_mode=pl.Buffered(1))


def _silu(v):
    return v * jax.nn.sigmoid(v)


def _bdot(a, b):
    return jnp.dot(a.astype(BF16), b.astype(BF16), preferred_element_type=F32)


def _bdot_nt(a, b):
    return lax.dot_general(a.astype(BF16), b.astype(BF16), (((1,), (1,)), ((), ())),
                           preferred_element_type=F32)


def _bdot_tn(a, b):
    return lax.dot_general(a.astype(BF16), b.astype(BF16), (((0,), (0,)), ((), ())),
                           preferred_element_type=F32)


def _rms(v):
    return v * lax.rsqrt(jnp.mean(v * v, axis=-1, keepdims=True) + EPS)


def _mod_kernel(c_ref, w_ref, b_ref, o_ref):
    sc = _silu(c_ref[...])
    o_ref[0, 0] = jnp.dot(sc, w_ref[0], precision=lax.Precision.HIGHEST,
                          preferred_element_type=F32) + b_ref[0]


def _mod_table(cvec, w_ada, b_ada):
    nm = cvec.shape[0]
    out = pl.pallas_call(
        _mod_kernel,
        grid=(DEPTH, N_MOD),
        in_specs=[
            pl.BlockSpec((nm, D_MODEL), lambda l, j: (0, 0)),
            pl.BlockSpec((1, D_MODEL, D_MODEL), lambda l, j: (l, 0, j)),
            pl.BlockSpec((1, 1, D_MODEL), lambda l, j: (l, 0, j)),
        ],
        out_specs=pl.BlockSpec((1, 1, nm, D_MODEL), lambda l, j: (l, j, 0, 0)),
        out_shape=jax.ShapeDtypeStruct((DEPTH, N_MOD, nm, D_MODEL), F32),
        compiler_params=_cparams(("parallel", "parallel")),
        name="mod_table",
    )(cvec, w_ada, b_ada.reshape(DEPTH, 1, N_MOD * D_MODEL))
    return jnp.transpose(out, (0, 2, 1, 3))


def _ffn_kernel(*refs, slot, gslot, n_first):
    if n_first is None:
        x_ref, mod_ref, ng_ref, w1_ref, w2_ref, o_ref = refs
        x = x_ref[...]
    else:
        x_ref, x2_ref, mod_ref, ng_ref, w1_ref, w2_ref, o_ref = refs
        x = jnp.where(pl.program_id(0) < n_first, x_ref[...], x2_ref[...])
    m = mod_ref[0]
    shift, scale, gate = m[slot:slot + 1], m[slot + 1:slot + 2], m[slot + 2:slot + 3]
    h = _rms(x) * ng_ref[gslot:gslot + 1, :]
    hb = (h * (1.0 + scale) + shift).astype(BF16)
    acc = None
    for c in range(D_FF // FF_CHUNK):
        lo = c * FF_CHUNK
        g = jnp.dot(hb, w1_ref[:, lo:lo + FF_CHUNK], preferred_element_type=F32)
        u = jnp.dot(hb, w1_ref[:, D_FF + lo:D_FF + lo + FF_CHUNK], preferred_element_type=F32)
        a = (_silu(g) * u).astype(BF16)
        part = jnp.dot(a, w2_ref[lo:lo + FF_CHUNK, :], preferred_element_type=F32)
        acc = part if acc is None else acc + part
    y = _rms(acc) * ng_ref[gslot + 1:gslot + 2, :]
    o_ref[...] = x + 0.5 * gate * y


def _half_ffn(s, mod, ng, w1, w2, *, slot, gslot, tm, n_tiles, tiles_per_sample, n_samples,
              tail=None):
    n_first = None if tail is None else s.shape[0] // tm
    rows_in = [pl.BlockSpec((tm, D_MODEL), lambda i: (i, 0))]
    args = [s]
    if tail is not None:
        rows_in = [pl.BlockSpec((tm, D_MODEL), lambda i: (jnp.minimum(i, n_first - 1), 0)),
                   pl.BlockSpec((tm, D_MODEL), lambda i: (jnp.maximum(i - n_first, 0), 0))]
        args = [s, tail]
    return pl.pallas_call(
        functools.partial(_ffn_kernel, slot=slot, gslot=gslot, n_first=n_first),
        grid=(n_tiles,),
        in_specs=rows_in + [
            pl.BlockSpec((1, N_MOD, D_MODEL),
                         lambda i: (jnp.minimum(i // tiles_per_sample, n_samples), 0, 0)),
            _const_spec(ng.shape),
            _const_spec(w1.shape),
            _const_spec(w2.shape),
        ],
        out_specs=pl.BlockSpec((tm, D_MODEL), lambda i: (i, 0)),
        out_shape=jax.ShapeDtypeStruct((n_tiles * tm, D_MODEL), F32),
        compiler_params=_cparams(("parallel",)),
        name="half_ffn",
    )(*args, mod, ng, w1, w2)


def _inproj_kernel(x_ref, mod_ref, ng_ref, w_ref, of_ref, oh_ref):
    m = mod_ref[0]
    h = _rms(x_ref[...]) * ng_ref[2:3, :]
    hb = (h * (1.0 + m[4:5]) + m[3:4]).astype(BF16)
    for lo in range(0, F_W, PROJ_TN):
        cs = slice(lo, min(lo + PROJ_TN, F_W))
        of_ref[:, cs] = jnp.dot(hb, w_ref[:, cs], preferred_element_type=F32)
    for j in range(H_W // PROJ_TN):
        cs = slice(j * PROJ_TN, (j + 1) * PROJ_TN)
        ws = slice(F_W + j * PROJ_TN, F_W + (j + 1) * PROJ_TN)
        oh_ref[:, cs] = jnp.dot(hb, w_ref[:, ws], preferred_element_type=F32).astype(BF16)


def _in_proj(s, mod, ng, w, *, tm, n_tiles, tiles_per_sample, n_samples):
    rows = n_tiles * tm
    return pl.pallas_call(
        _inproj_kernel,
        grid=(n_tiles,),
        in_specs=[
            pl.BlockSpec((tm, D_MODEL), lambda i: (i, 0)),
            pl.BlockSpec((1, N_MOD, D_MODEL),
                         lambda i: (jnp.minimum(i // tiles_per_sample, n_samples), 0, 0)),
            _const_spec(ng.shape),
            _const_spec(w.shape),
        ],
        out_specs=[
            pl.BlockSpec((tm, F_W), lambda i: (i, 0)),
            pl.BlockSpec((tm, H_W), lambda i: (i, 0)),
        ],
        out_shape=[jax.ShapeDtypeStruct((rows, F_W), F32),
                   jax.ShapeDtypeStruct((rows, H_W), BF16)],
        compiler_params=_cparams(("parallel",)),
        name="in_proj",
    )(s, mod, ng, w)


def _prep_kernel(x_ref, prev_ref, next_ref, small_ref, cw_ref, cb_ref, alog_ref, dtb_ref,
                 cvg_ref, cvs_ref, gb_ref, dt_ref, gt_ref, at_ref, *, nx_tiles,
                 x_tiles_per_seq, c_tiles_per_seq):
    i = pl.program_id(0)
    t = x_ref.shape[0]
    pos = jnp.where(i < nx_tiles, i % x_tiles_per_seq, (i - nx_tiles) % c_tiles_per_seq)
    last = jnp.where(i < nx_tiles, x_tiles_per_seq - 1, c_tiles_per_seq - 1)
    has_prev = (pos != 0).astype(F32)
    has_next = (pos != last).astype(F32)
    pad = CONV_K // 2

    for cc in range(CONV_W // 128):
        cs = slice(cc * 128, (cc + 1) * 128)
        ext = jnp.concatenate([prev_ref[:, cs] * has_prev, x_ref[:, cs],
                               next_ref[:, cs] * has_next], axis=0)
        y = cb_ref[:, cs] + cw_ref[pad:pad + 1, cs] * ext[8:8 + t]
        for k in range(CONV_K):
            if k != pad:
                shifted = pltpu.roll(ext, (pad - k) % (t + 16), 0)[8:8 + t]
                y = y + cw_ref[k:k + 1, cs] * shifted
        y = _silu(y)
        if cc < 2 * GDN_HEADS:
            y = y * lax.rsqrt(jnp.sum(y * y, axis=-1, keepdims=True) + EPS)
            if cc < GDN_HEADS:
                y = y * (GDN_DK ** -0.5)
        if cc < GDN_QKV // 128:
            cvg_ref[:, cs] = y
        else:
            cvs_ref[:, cc * 128 - GDN_QKV:(cc + 1) * 128 - GDN_QKV] = y

    sm = small_ref[...]
    lane = lax.broadcasted_iota(jnp.int32, (t, 128), 1)
    row = lax.broadcasted_iota(jnp.int32, (t, 128), 0)
    z = sm + dtb_ref[...]
    sp = jnp.maximum(z, 0.0) + jnp.log1p(jnp.exp(-jnp.abs(z)))
    g = -jnp.exp(alog_ref[...]) * sp
    is_gdn = lane < LANE_GB
    is_ssd = (lane >= LANE_DT) & (lane < LANE_DT + 2 * SSD_HEADS)
    win = jnp.where(is_gdn, GDN_CHUNK, SSD_CHUNK)
    rmod = row & (win - 1)
    g = jnp.where(is_gdn | is_ssd, g, 0.0)
    pre, suf = g, g
    sft = 1
    while sft < SSD_CHUNK:
        pre = pre + jnp.where(rmod >= sft, pltpu.roll(pre, sft, 0), 0.0)
        suf = suf + jnp.where(rmod < win - sft, pltpu.roll(suf, t - sft, 0), 0.0)
        sft *= 2
    backward = (is_gdn & (lane >= GDN_HEADS)) | (is_ssd & (lane >= LANE_DT + SSD_HEADS))
    cum = jnp.where(backward, suf, pre)
    is_beta = (lane >= LANE_GB) & (lane < LANE_DT)
    gb = jnp.where(is_beta, jax.nn.sigmoid(sm), cum)
    gb_ref[...] = gb
    dt_ref[...] = sp
    gbt = gb.T
    for c in range(t // GDN_CHUNK):
        gt_ref[c] = gbt[0:16, c * GDN_CHUNK:(c + 1) * GDN_CHUNK]
    for c in range(t // SSD_CHUNK):
        at_ref[c] = gbt[16:32, c * SSD_CHUNK:(c + 1) * SSD_CHUNK]


def _rope_kernel(xq_ref, xk_ref, v_ref, cos_ref, sin_ref, q_ref, kv_ref, *, nx_tiles):
    i = pl.program_id(0)
    is_x = i < nx_tiles
    cos, sin = cos_ref[...], sin_ref[...]
    for x_ref, o_ref, scale in ((xq_ref, q_ref, Q_SCALE), (xk_ref, kv_ref, None)):
        for b in range(DIFF_HEADS):
            bs = slice(b * 128, (b + 1) * 128)
            v = x_ref[:, bs]
            r = v * cos + pltpu.roll(v, 64, 1) * sin
            r = jnp.where(is_x, r, v)
            o_ref[:, bs] = (r if scale is None else r * scale).astype(BF16)
    kv_ref[:, 512:] = v_ref[...]


def _prep_rope_kernel(*refs, nx_tiles, x_tiles_per_seq, c_tiles_per_seq):
    _prep_kernel(*refs[:8], *refs[13:19], nx_tiles=nx_tiles, x_tiles_per_seq=x_tiles_per_seq,
                 c_tiles_per_seq=c_tiles_per_seq)
    _rope_kernel(*refs[8:13], *refs[19:21], nx_tiles=nx_tiles)


def _prep_rope(f, hh, cw, cb, alog, dtb, cos, sin, *, n_tiles, nx_tiles, x_tiles_per_seq,
               c_tiles_per_seq):
    t = SEQ_T
    rows = n_tiles * t
    r8 = rows // 8
    t8 = t // 8
    per_sample = x_tiles_per_seq + c_tiles_per_seq
    tab = pl.BlockSpec((t, 128), lambda i: (jnp.where(i < nx_tiles, i % x_tiles_per_seq, 0), 0))

    def kv_block(i):
        j = i - nx_tiles
        return jnp.where(i < nx_tiles,
                         (i // x_tiles_per_seq) * per_sample + i % x_tiles_per_seq,
                         (j // c_tiles_per_seq) * per_sample + x_tiles_per_seq + j % c_tiles_per_seq)

    return pl.pallas_call(
        functools.partial(_prep_rope_kernel, nx_tiles=nx_tiles, x_tiles_per_seq=x_tiles_per_seq,
                          c_tiles_per_seq=c_tiles_per_seq),
        grid=(n_tiles,),
        in_specs=[
            pl.BlockSpec((t, CONV_W), lambda i: (i, 0)),
            pl.BlockSpec((8, CONV_W), lambda i: (jnp.maximum(i * t8 - 1, 0), 0)),
            pl.BlockSpec((8, CONV_W), lambda i: (jnp.minimum((i + 1) * t8, r8 - 1), 0)),
            pl.BlockSpec((t, 128), lambda i: (i, F_SMALL // 128)),
            _const_spec(cw.shape),
            _const_spec(cb.shape),
            _const_spec(alog.shape),
            _const_spec(dtb.shape),
            pl.BlockSpec((t, 512), lambda i: (i, F_DQ // 512)),
            pl.BlockSpec((t, 512), lambda i: (i, F_DQ // 512 + 1)),
            pl.BlockSpec((t, 512), lambda i: (i, H_DV // 512)), tab, tab,
        ],
        out_specs=[
            pl.BlockSpec((t, GDN_QKV), lambda i: (i, 0)),
            pl.BlockSpec((t, SSD_XBC), lambda i: (i, 0)),
            pl.BlockSpec((t, 128), lambda i: (i, 0)),
            pl.BlockSpec((t, 128), lambda i: (i, 0)),
            pl.BlockSpec((t // GDN_CHUNK, 16, GDN_CHUNK), lambda i: (i, 0, 0)),
            pl.BlockSpec((t // SSD_CHUNK, 16, SSD_CHUNK), lambda i: (i, 0, 0)),
            pl.BlockSpec((t, 512), lambda i: (i, 0)),
            pl.BlockSpec((t, 1024), lambda i: (kv_block(i), 0)),
        ],
        out_shape=[
            jax.ShapeDtypeStruct((rows, GDN_QKV), F32),
            jax.ShapeDtypeStruct((rows, SSD_XBC), F32),
            jax.ShapeDtypeStruct((rows, 128), F32),
            jax.ShapeDtypeStruct((rows, 128), F32),
            jax.ShapeDtypeStruct((rows // GDN_CHUNK, 16, GDN_CHUNK), F32),
            jax.ShapeDtypeStruct((rows // SSD_CHUNK, 16, SSD_CHUNK), F32),
            jax.ShapeDtypeStruct((rows, 512), BF16),
            jax.ShapeDtypeStruct((rows, 1024), BF16),
        ],
        compiler_params=_cparams(("parallel",)),
        name="conv_prep_rope",
    )(f, f, f, f, cw, cb, alog, dtb, f, f, hh, cos, sin)


def _when(cond, fn):
    if isinstance(cond, bool):
        if cond:
            fn()
    else:
        pl.when(cond)(fn)


def _attn_kernel(lam_ref, nw_ref, q_ref, k_ref, v_ref, *rest, tq, tk, n_q, n_k, lam_init):
    o_ref, s0, s1, t0, t1, acc_scr, m_scr = rest[-7:]
    s_scr, mt_scr = (s0, s1), (t0, t1)
    half0 = (lax.broadcasted_iota(jnp.int32, (tq, 128), 1) & (DIFF_DQK // 2)) == 0
    ones_col = (lax.broadcasted_iota(jnp.int32, (tk, 128), 1) == 0).astype(BF16)
    n_steps = n_q * n_k

    def rows_of(i, size):
        r0 = i * size
        if not isinstance(i, int):
            r0 = pl.multiple_of(r0, size)
        return pl.ds(r0, size)

    def split(n):
        return (n // n_k, n % n_k)

    def scores(n, slot):
        qi, t = split(n)
        q = q_ref[rows_of(qi, tq), :]
        zero = jnp.zeros_like(q)
        k = k_ref[rows_of(t, tk), :]
        for j, keep in enumerate((half0, jnp.logical_not(half0))):
            s = lax.dot_general(jnp.where(keep, q, zero), k, (((1,), (1,)), ((), ())),
                                preferred_element_type=F32)
            s_scr[slot][j] = s
            mt_scr[slot][j] = jnp.broadcast_to(jnp.max(s, axis=1, keepdims=True), (tq, 128))

    def consume(n, slot):
        _, t = split(n)
        v = jnp.concatenate([v_ref[rows_of(t, tk), :], ones_col], axis=1)
        m_prevs = [m_scr[0], m_scr[1]]
        accs = [acc_scr[0], acc_scr[1]]
        for j in range(2):
            m_new = jnp.maximum(m_prevs[j], mt_scr[slot][j])
            m_scr[j] = m_new
            alpha = jnp.exp2(m_prevs[j] - m_new)
            p = jnp.exp2((s_scr[slot][j]
                          - jnp.concatenate([m_new] * (tk // 128), axis=1)).astype(BF16))
            acc_scr[j] = (jnp.concatenate([alpha, alpha], axis=1) * accs[j]
                          + jnp.dot(p, v, preferred_element_type=F32))

    def finalize(qi):
        lam = lam_ref[...]
        lam_full = (jnp.exp(jnp.sum(lam[0:1] * lam[1:2], axis=-1, keepdims=True))
                    - jnp.exp(jnp.sum(lam[2:3] * lam[3:4], axis=-1, keepdims=True)) + lam_init)
        a0, a1 = acc_scr[0], acc_scr[1]
        o = (a0[:, :DIFF_DV] / a0[:, DIFF_DV:DIFF_DV + 1]
             - lam_full * (a1[:, :DIFF_DV] / a1[:, DIFF_DV:DIFF_DV + 1]))
        o_ref[rows_of(qi, tq), :] = (_rms(o) * nw_ref[...] * (1.0 - lam_init)).astype(BF16)

    def reset_max():
        m_scr[...] = jnp.full(m_scr.shape, -jnp.inf, F32)

    def region(n, slot, last=False):
        qi, t = split(n)
        _when(t == 0, reset_max)
        if not last:
            scores(n + 1, 1 - slot)
        consume(n, slot)
        _when(t == n_k - 1, lambda: finalize(qi))

    acc_scr[...] = jnp.zeros_like(acc_scr)
    scores(0, 0)
    n_pair = (n_steps - 1) // 2

    def body(u, carry):
        region(2 * u, 0)
        region(2 * u + 1, 1)
        return carry

    if n_pair > 0:
        lax.fori_loop(0, n_pair, body, 0)
    for n in range(2 * n_pair, n_steps):
        region(n, n % 2, last=n == n_steps - 1)


def _attn_scratch(tq, tk):
    return ([pltpu.VMEM((2, tq, tk), F32)] * 2 + [pltpu.VMEM((2, tq, 128), F32)] * 2
            + [pltpu.VMEM((2, tq, 2 * DIFF_DV), F32), pltpu.VMEM((2, tq, 128), F32)])


def _attention(q, kv, lam, nw, *, bsz, seq, ctx_len, lam_init, with_ctx):
    tq = 1024 if seq % 1024 == 0 else 512
    n_keys = seq + ctx_len
    tk = 768 if n_keys % 768 == 0 else ctx_len
    rows = bsz * n_keys
    out = pl.pallas_call(
        functools.partial(_attn_kernel, tq=tq, tk=tk, n_q=seq // tq, n_k=n_keys // tk,
                          lam_init=lam_init),
        grid=(bsz, DIFF_HEADS),
        in_specs=[
            _const_spec(lam.shape),
            _const_spec(nw.shape),
            pl.BlockSpec((seq, 128), lambda b, h: (b, h)),
            pl.BlockSpec((n_keys, 128), lambda b, h: (b, h)),
            pl.BlockSpec((n_keys, 128), lambda b, h: (b, DIFF_HEADS + h)),
        ],
        out_specs=pl.BlockSpec((seq, 128), lambda b, h: (b, h)),
        out_shape=jax.ShapeDtypeStruct((rows, DIFF_HEADS * DIFF_DV), BF16),
        scratch_shapes=_attn_scratch(tq, tk),
        compiler_params=_cparams(("parallel", "parallel")),
        name="diff_attn",
    )(lam, nw, q, kv, kv)
    if not with_ctx:
        return out
    qblk = bsz * seq // ctx_len
    per_sample = n_keys // ctx_len

    def ctx_keys(col):
        return lambda b, h: (b * per_sample + per_sample - 1, col + h)

    return pl.pallas_call(
        functools.partial(_attn_kernel, tq=ctx_len, tk=ctx_len, n_q=1, n_k=1, lam_init=lam_init),
        grid=(bsz, DIFF_HEADS),
        in_specs=[
            _const_spec(lam.shape),
            _const_spec(nw.shape),
            pl.BlockSpec((ctx_len, 128), lambda b, h: (qblk + b, h)),
            pl.BlockSpec((ctx_len, 128), ctx_keys(0)),
            pl.BlockSpec((ctx_len, 128), ctx_keys(DIFF_HEADS)),
            pl.BlockSpec(memory_space=pl.ANY),
        ],
        out_specs=pl.BlockSpec((ctx_len, 128), lambda b, h: (qblk + b, h)),
        out_shape=jax.ShapeDtypeStruct((rows, DIFF_HEADS * DIFF_DV), BF16),
        scratch_shapes=_attn_scratch(ctx_len, ctx_len),
        input_output_aliases={5: 0},
        compiler_params=_cparams(("parallel", "parallel")),
        name="diff_attn_ctx",
    )(lam, nw, q, kv, kv, out)


def _seq_maps(bsz, nsx, nsc):
    cbase = bsz * nsx

    def fwd(b, s):
        return jnp.where(s < nsc, cbase + b * nsc + s, b * nsx + (s - nsc))

    def bwd(b, s):
        return jnp.where(s < nsc, cbase + b * nsc + (nsc - 1 - s), b * nsx + (nsx - 1 - (s - nsc)))

    return fwd, bwd


def _gdn_kernel(cvf_ref, gbf_ref, gtf_ref, cvb_ref, gbb_ref, gtb_ref, of_ref, ob_ref, st_ref):
    @pl.when(pl.program_id(1) == 0)
    def _():
        st_ref[...] = jnp.zeros_like(st_ref)

    c_ = GDN_CHUNK
    nck = cvf_ref.shape[0] // c_
    ii = lax.broadcasted_iota(jnp.int32, (c_, c_), 0)
    jj = lax.broadcasted_iota(jnp.int32, (c_, c_), 1)
    eye = (ii == jj).astype(F32)
    blk16 = (ii // 16) == (jj // 16)
    same32 = (ii // 32) == (jj // 32)
    lvl32 = same32 & jnp.logical_not(blk16)
    lvl64 = jnp.logical_not(same32)
    dirs = (
        (cvf_ref, gbf_ref, gtf_ref, of_ref, ii >= jj, ii > jj, c_ - 1),
        (cvb_ref, gbb_ref, gtb_ref, ob_ref, ii <= jj, ii < jj, 0),
    )

    class _P:
        pass

    probs = []
    for d, (cv_ref, gb_ref, gt_ref, o_ref, incl, strict, last_row) in enumerate(dirs):
        for c in range(nck):
            cc = c if d == 0 else nck - 1 - c
            for h in range(GDN_HEADS):
                p = _P()
                p.d, p.h, p.c, p.cc = d, h, c, cc
                p.cv, p.gb, p.gt, p.o = cv_ref, gb_ref, gt_ref, o_ref
                p.incl, p.strict, p.last_row = incl, strict, last_row
                p.rows = slice(cc * c_, (cc + 1) * c_)
                p.idx = d * GDN_HEADS + h
                probs.append(p)

    def q_of(p):
        return p.cv[p.rows, p.h * GDN_DK:(p.h + 1) * GDN_DK]

    def k_of(p):
        return p.cv[p.rows, 512 + p.h * GDN_DK:512 + (p.h + 1) * GDN_DK]

    def v_of(p):
        return p.cv[p.rows, 1024 + p.h * GDN_DV:1024 + (p.h + 1) * GDN_DV]

    def gcol_of(p):
        return p.gb[p.rows, :][:, LANE_GA + p.idx:LANE_GA + p.idx + 1]

    def beta_of(p):
        return p.gb[p.rows, :][:, LANE_GB + p.idx:LANE_GB + p.idx + 1]

    def glast_of(p):
        r = p.last_row
        return p.gb[p.rows, :][r:r + 1, LANE_GA + p.idx:LANE_GA + p.idx + 1]

    for p in probs:
        kh = k_of(p)
        grow = p.gt[p.cc][p.idx:p.idx + 1, :]
        dec = jnp.where(p.incl, jnp.exp(jnp.minimum(gcol_of(p) - grow, 0.0)), 0.0)
        ga = _bdot_nt(jnp.concatenate([kh * beta_of(p), q_of(p)], axis=0), kh)
        p.m = jnp.where(p.strict, ga[:c_] * dec, 0.0)
        p.aqk = ga[c_:] * dec
    for p in probs:
        mb = jnp.where(blk16, p.m, 0.0)
        p.tinv = eye - mb
        p.sq = _bdot(mb, mb)
    for step in range(3):
        for p in probs:
            p.tinv = p.tinv + _bdot(p.tinv, p.sq)
            if step < 2:
                p.sq = _bdot(p.sq, p.sq)
    for lvl in (lvl32, lvl64):
        for p in probs:
            p.x = _bdot(jnp.where(lvl, p.m, 0.0), p.tinv)
        for p in probs:
            p.tinv = p.tinv - _bdot(p.tinv, p.x)
    for p in probs:
        beta = beta_of(p)
        rhs = jnp.concatenate([v_of(p) * beta, k_of(p) * (beta * jnp.exp(gcol_of(p)))], axis=1)
        p.uw = _bdot(p.tinv, rhs)
    states = [[st_ref[d, h] for h in range(GDN_HEADS)] for d in range(2)]
    for c in range(nck):
        cur = [p for p in probs if p.c == c]
        for p in cur:
            lhs = jnp.concatenate([p.uw[:, GDN_DV:], q_of(p) * jnp.exp(gcol_of(p))], axis=0)
            p.wq = _bdot(lhs, states[p.d][p.h])
        for p in cur:
            p.v_new = p.uw[:, :GDN_DV] - p.wq[:c_]
            glast = glast_of(p)
            k_dec = k_of(p) * jnp.exp(glast - gcol_of(p))
            states[p.d][p.h] = (states[p.d][p.h] * jnp.exp(glast)
                                + _bdot_tn(k_dec, p.v_new))
        for p in cur:
            p.out = p.wq[c_:] + _bdot(p.aqk, p.v_new)
    for p in probs:
        p.o[p.rows, p.h * GDN_DV:(p.h + 1) * GDN_DV] = p.out
    for d in range(2):
        for h in range(GDN_HEADS):
            st_ref[d, h] = states[d][h]


def _gdn(cv, gb, gt, *, bsz, nsx, nsc):
    t = SEQ_T
    rows = cv.shape[0]
    fwd, bwd = _seq_maps(bsz, nsx, nsc)
    nck = t // GDN_CHUNK

    def specs(fn):
        return [
            pl.BlockSpec((t, GDN_QKV), lambda b, s: (fn(b, s), 0)),
            pl.BlockSpec((t, 128), lambda b, s: (fn(b, s), 0)),
            pl.BlockSpec((nck, 16, GDN_CHUNK), lambda b, s: (fn(b, s), 0, 0)),
        ]

    return pl.pallas_call(
        _gdn_kernel,
        grid=(bsz, nsc + nsx),
        in_specs=specs(fwd) + specs(bwd),
        out_specs=[pl.BlockSpec((t, 512), lambda b, s: (fwd(b, s), 0)),
                   pl.BlockSpec((t, 512), lambda b, s: (bwd(b, s), 0))],
        out_shape=[jax.ShapeDtypeStruct((rows, 512), F32)] * 2,
        scratch_shapes=[pltpu.VMEM((2, GDN_HEADS, GDN_DK, GDN_DV), F32)],
        compiler_params=_cparams(("parallel", "arbitrary")),
        name="gdn_scan",
    )(cv, gb, gt, cv, gb, gt)


def _split_dot(x, sel):
    hi = x.astype(BF16)
    lo = (x - hi.astype(F32)).astype(BF16)
    return (jnp.dot(hi, sel, preferred_element_type=F32)
            + jnp.dot(lo, sel, preferred_element_type=F32))


def _ssd_kernel(dsk_ref, sel64_ref, cvf_ref, gbf_ref, dtf_ref, atf_ref, cvb_ref,
                gbb_ref, dtb_ref, atb_ref, yf_ref, yb_ref, st_ref):
    @pl.when(pl.program_id(1) == 0)
    def _():
        st_ref[...] = jnp.zeros_like(st_ref)

    c_ = SSD_CHUNK
    p_ = SSD_HEADDIM
    hpg = SSD_HEADS // SSD_GROUPS
    gw = hpg * p_
    nck = cvf_ref.shape[0] // c_
    ii = lax.broadcasted_iota(jnp.int32, (c_, c_), 0)
    jj = lax.broadcasted_iota(jnp.int32, (c_, c_), 1)
    first_half = lax.broadcasted_iota(jnp.int32, (c_, 2 * p_), 1) < p_
    dirs = (
        (cvf_ref, gbf_ref, dtf_ref, atf_ref, yf_ref, ii >= jj, c_ - 1),
        (cvb_ref, gbb_ref, dtb_ref, atb_ref, yb_ref, ii <= jj, 0),
    )

    class _P:
        pass

    probs = []
    for d, (cv_ref, gb_ref, dt_ref, at_ref, y_ref, incl, last_row) in enumerate(dirs):
        for c in range(nck):
            p = _P()
            p.d, p.c = d, c
            p.cc = c if d == 0 else nck - 1 - c
            p.rows = slice(p.cc * c_, (p.cc + 1) * c_)
            p.cv, p.y, p.incl, p.last_row = cv_ref, y_ref, incl, last_row
            p.at = at_ref[p.cc]
            p.acs, dts = gb_ref[p.rows, :], dt_ref[p.rows, :]
            p.a64 = _split_dot(p.acs, sel64_ref[d])
            xs = cv_ref[p.rows, 0:SSD_INNER]
            p.xdt = xs * _split_dot(dts, sel64_ref[d])
            alast = p.a64[last_row:last_row + 1, :]
            p.xd = p.xdt * jnp.exp(alast - p.a64)
            p.sdec = jnp.exp(alast)
            p.skip = dsk_ref[...] * xs if d == 0 else None
            probs.append(p)

    def b_of(p, g):
        return p.cv[p.rows, SSD_INNER + g * SSD_STATE:SSD_INNER + (g + 1) * SSD_STATE]

    def c_of(p, g):
        return p.cv[p.rows, SSD_INNER + (SSD_GROUPS + g) * SSD_STATE:
                    SSD_INNER + (SSD_GROUPS + g + 1) * SSD_STATE]

    states = [[st_ref[d, g] for g in range(SSD_GROUPS)] for d in range(2)]

    def chain_read(c):
        for p in probs:
            if p.c == c:
                p.y_off = [_bdot(c_of(p, g), states[p.d][g]) for g in range(SSD_GROUPS)]

    def chain_write(c):
        for p in probs:
            if p.c == c:
                for g in range(SSD_GROUPS):
                    gs = slice(g * gw, (g + 1) * gw)
                    states[p.d][g] = (states[p.d][g] * p.sdec[:, gs]
                                      + _bdot_tn(b_of(p, g), p.xd[:, gs]))

    def diag(c):
        for p in probs:
            if p.c != c:
                continue
            p.y_diag = []
            for g in range(SSD_GROUPS):
                cb = p.cb[g]
                for pair in range(hpg // 2):
                    lo_h = g * hpg + 2 * pair
                    xpair = p.xdt[:, lo_h * p_:(lo_h + 2) * p_]
                    halves = []
                    for h in (lo_h, lo_h + 1):
                        arow = p.at[p.d * SSD_HEADS + h:p.d * SSD_HEADS + h + 1, :]
                        ln = LANE_DT + p.d * SSD_HEADS + h
                        acol = p.acs[:, ln:ln + 1]
                        ldec = jnp.where(p.incl, jnp.exp(jnp.minimum(acol - arow, 0.0)), 0.0)
                        halves.append(_bdot(cb * ldec, xpair))
                    p.y_diag.append(jnp.where(first_half, halves[0], halves[1]))

    chain_read(0)
    for p in probs:
        p.cb = [_bdot_nt(c_of(p, g), b_of(p, g)) for g in range(SSD_GROUPS)]
    chain_write(0)
    diag(0)
    for c in range(1, nck):
        chain_read(c)
        diag(c)
        chain_write(c)
    for p in probs:
        y = jnp.concatenate(p.y_diag, axis=1) + jnp.concatenate(p.y_off, axis=1) * jnp.exp(p.a64)
        if p.skip is not None:
            y = y + p.skip
        p.y[p.rows, :] = y
    for d in range(2):
        for g in range(SSD_GROUPS):
            st_ref[d, g] = states[d][g]


def _ssd(dsk, sel64, cv, gb, dt, at, *, bsz, nsx, nsc):
    t = SEQ_T
    rows = cv.shape[0]
    fwd, bwd = _seq_maps(bsz, nsx, nsc)
    nck = t // SSD_CHUNK

    def specs(fn):
        return [
            pl.BlockSpec((t, SSD_XBC), lambda b, s: (fn(b, s), 0)),
            pl.BlockSpec((t, 128), lambda b, s: (fn(b, s), 0)),
            pl.BlockSpec((t, 128), lambda b, s: (fn(b, s), 0)),
            pl.BlockSpec((nck, 16, SSD_CHUNK), lambda b, s: (fn(b, s), 0, 0)),
        ]

    return pl.pallas_call(
        _ssd_kernel,
        grid=(bsz, nsc + nsx),
        in_specs=([_const_spec(dsk.shape), _const_spec(sel64.shape)]
                  + specs(fwd) + specs(bwd)),
        out_specs=[pl.BlockSpec((t, SSD_INNER), lambda b, s: (fwd(b, s), 0)),
                   pl.BlockSpec((t, SSD_INNER), lambda b, s: (bwd(b, s), 0))],
        out_shape=[jax.ShapeDtypeStruct((rows, SSD_INNER), F32)] * 2,
        scratch_shapes=[pltpu.VMEM((2, SSD_GROUPS, SSD_STATE, SSD_INNER // SSD_GROUPS), F32)],
        compiler_params=_cparams(("parallel", "arbitrary")),
        name="ssd_scan",
    )(dsk, sel64, cv, gb, dt, at, cv, gb, dt, at)


def _merge_kernel(x_ref, mod_ref, ng_ref, gof_ref, gob_ref, gz_ref, att_ref, syf_ref, syb_ref,
                  sz_ref, gate_ref, gnw_ref, snw_ref, wb_ref, wo_ref, o_ref):
    oa = gof_ref[...] + gob_ref[...]
    parts = []
    for h in range(GDN_HEADS):
        hs = slice(h * GDN_DV, (h + 1) * GDN_DV)
        parts.append(_rms(oa[:, hs]) * gnw_ref[:, hs])
    oa = jnp.concatenate(parts, axis=1) * _silu(gz_ref[...].astype(F32))
    oc = (syf_ref[...] + syb_ref[...]) * _silu(sz_ref[...].astype(F32))
    oc = _rms(oc) * snw_ref[...]
    outs = (oa.astype(BF16), att_ref[...], oc.astype(BF16))
    y = None
    for n in range(N_BRANCH):
        gate = jax.nn.sigmoid(gate_ref[:, n * D_MODEL:(n + 1) * D_MODEL].astype(F32))
        term = gate * jnp.dot(outs[n], wb_ref[n], preferred_element_type=F32)
        y = term if y is None else y + term
    y = jnp.dot(y.astype(BF16), wo_ref[...], preferred_element_type=F32)
    m = mod_ref[0]
    o_ref[...] = x_ref[...] + m[5:6] * (_rms(y) * ng_ref[3:4, :])


def _merge(s, mod, ng, gof, gob, hh, att, syf, syb, gnw, snw, wb, wo, *, tm, n_tiles,
           tiles_per_sample, n_samples):
    rows = n_tiles * tm
    row512 = pl.BlockSpec((tm, 512), lambda i: (i, 0))
    return pl.pallas_call(
        _merge_kernel,
        grid=(n_tiles,),
        in_specs=[
            pl.BlockSpec((tm, D_MODEL), lambda i: (i, 0)),
            pl.BlockSpec((1, N_MOD, D_MODEL),
                         lambda i: (jnp.minimum(i // tiles_per_sample, n_samples), 0, 0)),
            _const_spec(ng.shape),
            row512, row512,
            pl.BlockSpec((tm, 512), lambda i: (i, H_GZ // 512)),
            row512, row512, row512,
            pl.BlockSpec((tm, 512), lambda i: (i, H_SZ // 512)),
            pl.BlockSpec((tm, N_BRANCH * D_MODEL), lambda i: (i, 0)),
            _const_spec(gnw.shape),
            _const_spec(snw.shape),
            _const_spec(wb.shape),
            _const_spec(wo.shape),
        ],
        out_specs=pl.BlockSpec((tm, D_MODEL), lambda i: (i, 0)),
        out_shape=jax.ShapeDtypeStruct((rows, D_MODEL), F32),
        compiler_params=_cparams(("parallel",)),
        name="merge",
    )(s, mod, ng, gof, gob, hh, att, syf, syb, hh, hh, gnw, snw, wb, wo)


def _deinterleave_pairs(w):
    d = w.shape[0]
    w = w.reshape(d, DIFF_HEADS, 2, DIFF_DQK // 2, 2)
    return jnp.transpose(w, (0, 1, 4, 2, 3)).reshape(d, DIFF_HEADS * 2 * DIFF_DQK)


def _proj_weight(w_in):
    w_in = w_in.astype(BF16)
    o = 0
    parts = {}
    for name, width in (("gqkv", GDN_QKV), ("gz", 512), ("ga", 8), ("gb", 8), ("dq", 512),
                        ("dk", 512), ("dv", 512), ("sz", 512), ("sxbc", SSD_XBC), ("sdt", 16),
                        ("gate", N_BRANCH * D_MODEL)):
        parts[name] = w_in[:, o:o + width]
        o += width
    small_pad = jnp.zeros((w_in.shape[0], SMALL_W - 32), w_in.dtype)
    cols = [parts["gqkv"], parts["sxbc"],
            _deinterleave_pairs(parts["dq"]), _deinterleave_pairs(parts["dk"]),
            parts["ga"], parts["gb"], parts["sdt"], small_pad,
            parts["gate"], parts["dv"], parts["gz"], parts["sz"]]
    return jnp.concatenate(cols, axis=1)


def _rope_tables(seq):
    rows = seq // GRID_W
    row = jnp.repeat(jnp.arange(rows, dtype=F32), GRID_W)
    col = jnp.tile(jnp.arange(GRID_W, dtype=F32), rows)
    inv = ROPE_BASE ** (-jnp.arange(ROPE_PAIRS_PER_AXIS, dtype=F32) / ROPE_PAIRS_PER_AXIS)
    ang = jnp.concatenate([row[:, None] * inv, col[:, None] * inv], axis=-1)
    cos, sin = jnp.cos(ang), jnp.sin(ang)
    return jnp.tile(cos, (1, 4)), jnp.concatenate([-sin, -sin, sin, sin], axis=1)


def _ssd_selector():
    rows = jnp.arange(128)[None, :, None]
    cols = jnp.arange(SSD_INNER)[None, None, :]
    d = jnp.arange(2)[:, None, None]
    return (rows == LANE_DT + d * SSD_HEADS + cols // SSD_HEADDIM).astype(BF16)


def _lane_vec(pieces):
    v = jnp.zeros((128,), F32)
    for off, val in pieces:
        v = lax.dynamic_update_slice(v, val.reshape(-1).astype(F32), (off,))
    return v.reshape(1, 128)


def kernel(x, c, ctx, c_ctx, w_ada, b_ada, norm_g, w_ffn_in, w_ffn_out, w_in, gdn_conv_w, gdn_a_log,
           gdn_dt_bias, gdn_norm_w, diff_lambda, diff_norm_w, ssd_conv_w, ssd_conv_b, ssd_a_log,
           ssd_dt_bias, ssd_d, ssd_norm_w, w_branch, w_out):
    bsz, seq, _ = x.shape
    ctx_len = ctx.shape[1]
    t = SEQ_T
    tm = 512
    assert ctx_len == t and seq % 512 == 0 and (bsz * ctx_len) % tm == 0 and seq % GRID_W == 0
    nx_rows, nc_rows = bsz * seq, bsz * ctx_len
    n_rows = nx_rows + nc_rows

    nm = -(-(bsz + 1) // 8) * 8
    cvec = jnp.zeros((nm, D_MODEL), F32).at[:bsz].set(c).at[bsz].set(c_ctx)
    mod_all = _mod_table(cvec, w_ada, b_ada)
    cos, sin = _rope_tables(seq)
    sel64 = _ssd_selector()

    tok = dict(tm=tm, tiles_per_sample=seq // tm, n_samples=bsz)
    all_tiles, x_tiles = n_rows // tm, nx_rows // tm
    seq_kw = dict(n_tiles=n_rows // t, nx_tiles=nx_rows // t, x_tiles_per_seq=seq // t)
    nsx, nsc = seq // t, ctx_len // t

    s = None
    for i in range(DEPTH):
        last = i == DEPTH - 1
        lam_init = 0.8 - 0.6 * math.exp(-0.3 * i)
        mod, ng = mod_all[i], norm_g[i]
        w1 = [w_ffn_in[i, j].astype(BF16) for j in range(2)]
        w2 = [w_ffn_out[i, j].astype(BF16) for j in range(2)]

        tail = ctx.reshape(nc_rows, D_MODEL) if i == 0 else None
        s = _half_ffn(x.reshape(nx_rows, D_MODEL) if i == 0 else s, mod, ng, w1[0], w2[0], slot=0,
                      gslot=0, n_tiles=all_tiles, tail=tail, **tok)

        f, hh = _in_proj(s, mod, ng, _proj_weight(w_in[i]), n_tiles=all_tiles, **tok)
        cw = jnp.concatenate([gdn_conv_w[i], ssd_conv_w[i]], axis=1)
        cb = jnp.concatenate([jnp.zeros((GDN_QKV,), F32), ssd_conv_b[i]]).reshape(1, CONV_W)
        alog = _lane_vec([(LANE_GA, gdn_a_log[i]), (LANE_DT, ssd_a_log[i])])
        dtb = _lane_vec([(LANE_GA, gdn_dt_bias[i]), (LANE_DT, ssd_dt_bias[i])])
        cvg, cvs, gb, dt, gt, at, qr, kv = _prep_rope(f, hh, cw, cb, alog, dtb, cos, sin,
                                                      c_tiles_per_seq=ctx_len // t, **seq_kw)

        gof, gob = _gdn(cvg, gb, gt, bsz=bsz, nsx=nsx, nsc=nsc)
        dsk = jnp.repeat(ssd_d[i].astype(F32), SSD_HEADDIM).reshape(1, SSD_INNER)
        syf, syb = _ssd(dsk, sel64, cvs, gb, dt, at, bsz=bsz, nsx=nsx, nsc=nsc)
        att = _attention(qr, kv, diff_lambda[i], diff_norm_w[i].reshape(1, DIFF_DV), bsz=bsz,
                         seq=seq, ctx_len=ctx_len, lam_init=lam_init, with_ctx=not last)

        n_out = x_tiles if last else all_tiles
        gnw = jnp.tile(gdn_norm_w[i], GDN_HEADS).reshape(1, GDN_HEADS * GDN_DV)
        snw = ssd_norm_w[i].reshape(1, SSD_INNER)
        s = _merge(s, mod, ng, gof, gob, hh, att, syf, syb, gnw, snw, w_branch[i].astype(BF16),
                   w_out[i].astype(BF16), n_tiles=n_out, **tok)
        s = _half_ffn(s, mod, ng, w1[1], w2[1], slot=6, gslot=4, n_tiles=n_out, **tok)

    return s[:nx_rows].reshape(bsz, seq, D_MODEL)
```

```python
import functools
import math

import jax
import jax.numpy as jnp
from jax import lax
from jax.experimental import pallas as pl
from jax.experimental.pallas import tpu as pltpu

F32 = jnp.float32
BF16 = jnp.bfloat16

D_MODEL = 1024
DEPTH = 2
GRID_W = 64
EPS = 1e-6
N_MOD = 9
D_FF = 2816
CONV_K = 5
GDN_HEADS = 4
GDN_DK = 128
GDN_DV = 128
GDN_CHUNK = 64
DIFF_HEADS = 4
DIFF_DQK = 64
DIFF_DV = 128
ROPE_BASE = 10000.0
ROPE_PAIRS_PER_AXIS = DIFF_DQK // 4
SSD_HEADS = 8
SSD_HEADDIM = 64
SSD_INNER = SSD_HEADS * SSD_HEADDIM
SSD_GROUPS = 2
SSD_STATE = 128
SSD_CHUNK = 128
N_BRANCH = 3
BRANCH_W = 512

GDN_QKV = 2 * GDN_HEADS * GDN_DK + GDN_HEADS * GDN_DV
SSD_XBC = SSD_INNER + 2 * SSD_GROUPS * SSD_STATE
CONV_W = GDN_QKV + SSD_XBC
SMALL_W = 128
F_DQ = CONV_W
F_SMALL = F_DQ + 2 * 512
F_W = F_SMALL + SMALL_W
H_DV = N_BRANCH * D_MODEL
H_GZ = H_DV + 512
H_SZ = H_GZ + 512
H_W = H_SZ + 512
LANE_GA = 0
LANE_GB = 8
LANE_DT = 16

Q_SCALE = DIFF_DQK ** -0.5 * math.log2(math.e)

FF_CHUNK = 256
PROJ_TN = 512
SEQ_T = 256
VMEM_LIMIT = 56 * 1024 * 1024


def _cparams(sem):
    return pltpu.CompilerParams(dimension_semantics=sem, vmem_limit_bytes=VMEM_LIMIT)


def _const_spec(shape):
    nd = len(shape)
    return pl.BlockSpec(shape, lambda *_: (0,) * nd, pipeline_mode=pl.Buffered(1))


def _silu(v):
    return v * jax.nn.sigmoid(v)


def _bdot(a, b):
    return jnp.dot(a.astype(BF16), b.astype(BF16), preferred_element_type=F32)


def _bdot_nt(a, b):
    return lax.dot_general(a.astype(BF16), b.astype(BF16), (((1,), (1,)), ((), ())),
                           preferred_element_type=F32)


def _bdot_tn(a, b):
    return lax.dot_general(a.astype(BF16), b.astype(BF16), (((0,), (0,)), ((), ())),
                           preferred_element_type=F32)


def _rms(v):
    return v * lax.rsqrt(jnp.mean(v * v, axis=-1, keepdims=True) + EPS)


def _mod_kernel(c_ref, w_ref, b_ref, o_ref):
    sc = _silu(c_ref[...])
    o_ref[0, 0] = jnp.dot(sc, w_ref[0], precision=lax.Precision.HIGHEST,
                          preferred_element_type=F32) + b_ref[0]


def _mod_table(cvec, w_ada, b_ada):
    nm = cvec.shape[0]
    out = pl.pallas_call(
        _mod_kernel,
        grid=(DEPTH, N_MOD),
        in_specs=[
            pl.BlockSpec((nm, D_MODEL), lambda l, j: (0, 0)),
            pl.BlockSpec((1, D_MODEL, D_MODEL), lambda l, j: (l, 0, j)),
            pl.BlockSpec((1, 1, D_MODEL), lambda l, j: (l, 0, j)),
        ],
        out_specs=pl.BlockSpec((1, 1, nm, D_MODEL), lambda l, j: (l, j, 0, 0)),
        out_shape=jax.ShapeDtypeStruct((DEPTH, N_MOD, nm, D_MODEL), F32),
        compiler_params=_cparams(("parallel", "parallel")),
        name="mod_table",
    )(cvec, w_ada, b_ada.reshape(DEPTH, 1, N_MOD * D_MODEL))
    return jnp.transpose(out, (0, 2, 1, 3))


def _ffn_kernel(*refs, slot, gslot, n_first):
    if n_first is None:
        x_ref, mod_ref, ng_ref, w1_ref, w2_ref, o_ref = refs
        x = x_ref[...]
    else:
        x_ref, x2_ref, mod_ref, ng_ref, w1_ref, w2_ref, o_ref = refs
        x = jnp.where(pl.program_id(0) < n_first, x_ref[...], x2_ref[...])
    m = mod_ref[0]
    shift, scale, gate = m[slot:slot + 1], m[slot + 1:slot + 2], m[slot + 2:slot + 3]
    h = _rms(x) * ng_ref[gslot:gslot + 1, :]
    hb = (h * (1.0 + scale) + shift).astype(BF16)
    acc = None
    for c in range(D_FF // FF_CHUNK):
        lo = c * FF_CHUNK
        g = jnp.dot(hb, w1_ref[:, lo:lo + FF_CHUNK], preferred_element_type=F32)
        u = jnp.dot(hb, w1_ref[:, D_FF + lo:D_FF + lo + FF_CHUNK], preferred_element_type=F32)
        a = (_silu(g) * u).astype(BF16)
        part = jnp.dot(a, w2_ref[lo:lo + FF_CHUNK, :], preferred_element_type=F32)
        acc = part if acc is None else acc + part
    y = _rms(acc) * ng_ref[gslot + 1:gslot + 2, :]
    o_ref[...] = x + 0.5 * gate * y


def _half_ffn(s, mod, ng, w1, w2, *, slot, gslot, tm, n_tiles, tiles_per_sample, n_samples,
              tail=None):
    n_first = None if tail is None else s.shape[0] // tm
    rows_in = [pl.BlockSpec((tm, D_MODEL), lambda i: (i, 0))]
    args = [s]
    if tail is not None:
        rows_in = [pl.BlockSpec((tm, D_MODEL), lambda i: (jnp.minimum(i, n_first - 1), 0)),
                   pl.BlockSpec((tm, D_MODEL), lambda i: (jnp.maximum(i - n_first, 0), 0))]
        args = [s, tail]
    return pl.pallas_call(
        functools.partial(_ffn_kernel, slot=slot, gslot=gslot, n_first=n_first),
        grid=(n_tiles,),
        in_specs=rows_in + [
            pl.BlockSpec((1, N_MOD, D_MODEL),
                         lambda i: (jnp.minimum(i // tiles_per_sample, n_samples), 0, 0)),
            _const_spec(ng.shape),
            _const_spec(w1.shape),
            _const_spec(w2.shape),
        ],
        out_specs=pl.BlockSpec((tm, D_MODEL), lambda i: (i, 0)),
        out_shape=jax.ShapeDtypeStruct((n_tiles * tm, D_MODEL), F32),
        compiler_params=_cparams(("parallel",)),
        name="half_ffn",
    )(*args, mod, ng, w1, w2)


def _inproj_kernel(x_ref, mod_ref, ng_ref, w_ref, of_ref, oh_ref):
    m = mod_ref[0]
    h = _rms(x_ref[...]) * ng_ref[2:3, :]
    hb = (h * (1.0 + m[4:5]) + m[3:4]).astype(BF16)
    for lo in range(0, F_W, PROJ_TN):
        cs = slice(lo, min(lo + PROJ_TN, F_W))
        of_ref[:, cs] = jnp.dot(hb, w_ref[:, cs], preferred_element_type=F32)
    for j in range(H_W // PROJ_TN):
        cs = slice(j * PROJ_TN, (j + 1) * PROJ_TN)
        ws = slice(F_W + j * PROJ_TN, F_W + (j + 1) * PROJ_TN)
        oh_ref[:, cs] = jnp.dot(hb, w_ref[:, ws], preferred_element_type=F32).astype(BF16)


def _in_proj(s, mod, ng, w, *, tm, n_tiles, tiles_per_sample, n_samples):
    rows = n_tiles * tm
    return pl.pallas_call(
        _inproj_kernel,
        grid=(n_tiles,),
        in_specs=[
            pl.BlockSpec((tm, D_MODEL), lambda i: (i, 0)),
            pl.BlockSpec((1, N_MOD, D_MODEL),
                         lambda i: (jnp.minimum(i // tiles_per_sample, n_samples), 0, 0)),
            _const_spec(ng.shape),
            _const_spec(w.shape),
        ],
        out_specs=[
            pl.BlockSpec((tm, F_W), lambda i: (i, 0)),
            pl.BlockSpec((tm, H_W), lambda i: (i, 0)),
        ],
        out_shape=[jax.ShapeDtypeStruct((rows, F_W), F32),
                   jax.ShapeDtypeStruct((rows, H_W), BF16)],
        compiler_params=_cparams(("parallel",)),
        name="in_proj",
    )(s, mod, ng, w)


def _prep_kernel(x_ref, prev_ref, next_ref, small_ref, cw_ref, cb_ref, alog_ref, dtb_ref, tri_ref,
                 cvg_ref, cvs_ref, gb_ref, dt_ref, gt_ref, at_ref, *, nx_tiles,
                 x_tiles_per_seq, c_tiles_per_seq):
    i = pl.program_id(0)
    t = x_ref.shape[0]
    pos = jnp.where(i < nx_tiles, i % x_tiles_per_seq, (i - nx_tiles) % c_tiles_per_seq)
    last = jnp.where(i < nx_tiles, x_tiles_per_seq - 1, c_tiles_per_seq - 1)
    has_prev = (pos != 0).astype(F32)
    has_next = (pos != last).astype(F32)
    pad = CONV_K // 2

    for cc in range(CONV_W // 128):
        cs = slice(cc * 128, (cc + 1) * 128)
        ext = jnp.concatenate([prev_ref[:, cs] * has_prev, x_ref[:, cs],
                               next_ref[:, cs] * has_next], axis=0)
        y = cb_ref[:, cs] + cw_ref[pad:pad + 1, cs] * ext[8:8 + t]
        for k in range(CONV_K):
            if k != pad:
                shifted = pltpu.roll(ext, (pad - k) % (t + 16), 0)[8:8 + t]
                y = y + cw_ref[k:k + 1, cs] * shifted
        y = _silu(y)
        if cc < 2 * GDN_HEADS:
            y = y * lax.rsqrt(jnp.sum(y * y, axis=-1, keepdims=True) + EPS)
            if cc < GDN_HEADS:
                y = y * (GDN_DK ** -0.5)
        if cc < GDN_QKV // 128:
            cvg_ref[:, cs] = y
        else:
            cvs_ref[:, cc * 128 - GDN_QKV:(cc + 1) * 128 - GDN_QKV] = y

    sm = small_ref[...]
    lane = lax.broadcasted_iota(jnp.int32, (t, 128), 1)
    z = sm + dtb_ref[...]
    sp = jnp.maximum(z, 0.0) + jnp.log1p(jnp.exp(-jnp.abs(z)))
    g = -jnp.exp(alog_ref[...]) * sp
    is_gdn = lane < LANE_GB
    is_ssd = (lane >= LANE_DT) & (lane < LANE_DT + 2 * SSD_HEADS)
    g = jnp.where(is_gdn | is_ssd, g, 0.0)
    hi = g.astype(BF16)
    r1 = g - hi.astype(F32)
    mid = r1.astype(BF16)
    lo = (r1 - mid.astype(F32)).astype(BF16)

    def tri_sum(n):
        tri = tri_ref[n]
        return (jnp.dot(tri, hi, preferred_element_type=F32)
                + jnp.dot(tri, mid, preferred_element_type=F32)
                + jnp.dot(tri, lo, preferred_element_type=F32))

    backward = (is_gdn & (lane >= GDN_HEADS)) | (is_ssd & (lane >= LANE_DT + SSD_HEADS))
    cum = jnp.where(is_gdn, jnp.where(backward, tri_sum(1), tri_sum(0)),
                    jnp.where(backward, tri_sum(3), tri_sum(2)))
    is_beta = (lane >= LANE_GB) & (lane < LANE_DT)
    gb = jnp.where(is_beta, jax.nn.sigmoid(sm), cum)
    gb_ref[...] = gb
    dt_ref[...] = sp
    gbt = gb.T
    for c in range(t // GDN_CHUNK):
        gt_ref[c] = gbt[0:16, c * GDN_CHUNK:(c + 1) * GDN_CHUNK]
    for c in range(t // SSD_CHUNK):
        at_ref[c] = gbt[16:32, c * SSD_CHUNK:(c + 1) * SSD_CHUNK]


def _rope_kernel(xq_ref, xk_ref, v_ref, cos_ref, sin_ref, q_ref, kv_ref, *, nx_tiles):
    i = pl.program_id(0)
    is_x = i < nx_tiles
    cos, sin = cos_ref[...], sin_ref[...]
    for x_ref, o_ref, scale in ((xq_ref, q_ref, Q_SCALE), (xk_ref, kv_ref, None)):
        for b in range(DIFF_HEADS):
            bs = slice(b * 128, (b + 1) * 128)
            v = x_ref[:, bs]
            r = v * cos + pltpu.roll(v, 64, 1) * sin
            r = jnp.where(is_x, r, v)
            o_ref[:, bs] = (r if scale is None else r * scale).astype(BF16)
    kv_ref[:, 512:] = v_ref[...]


def _prep_rope_kernel(*refs, nx_tiles, x_tiles_per_seq, c_tiles_per_seq):
    _prep_kernel(*refs[:9], *refs[14:20], nx_tiles=nx_tiles, x_tiles_per_seq=x_tiles_per_seq,
                 c_tiles_per_seq=c_tiles_per_seq)
    _rope_kernel(*refs[9:14], *refs[20:22], nx_tiles=nx_tiles)


def _prep_rope(f, hh, cw, cb, alog, dtb, tri, cos, sin, *, n_tiles, nx_tiles, x_tiles_per_seq,
               c_tiles_per_seq):
    t = SEQ_T
    rows = n_tiles * t
    r8 = rows // 8
    t8 = t // 8
    per_sample = x_tiles_per_seq + c_tiles_per_seq
    tab = pl.BlockSpec((t, 128), lambda i: (jnp.where(i < nx_tiles, i % x_tiles_per_seq, 0), 0))

    def kv_block(i):
        j = i - nx_tiles
        return jnp.where(i < nx_tiles,
                         (i // x_tiles_per_seq) * per_sample + i % x_tiles_per_seq,
                         (j // c_tiles_per_seq) * per_sample + x_tiles_per_seq + j % c_tiles_per_seq)

    return pl.pallas_call(
        functools.partial(_prep_rope_kernel, nx_tiles=nx_tiles, x_tiles_per_seq=x_tiles_per_seq,
                          c_tiles_per_seq=c_tiles_per_seq),
        grid=(n_tiles,),
        in_specs=[
            pl.BlockSpec((t, CONV_W), lambda i: (i, 0)),
            pl.BlockSpec((8, CONV_W), lambda i: (jnp.maximum(i * t8 - 1, 0), 0)),
            pl.BlockSpec((8, CONV_W), lambda i: (jnp.minimum((i + 1) * t8, r8 - 1), 0)),
            pl.BlockSpec((t, 128), lambda i: (i, F_SMALL // 128)),
            _const_spec(cw.shape),
            _const_spec(cb.shape),
            _const_spec(alog.shape),
            _const_spec(dtb.shape),
            _const_spec(tri.shape),
            pl.BlockSpec((t, 512), lambda i: (i, F_DQ // 512)),
            pl.BlockSpec((t, 512), lambda i: (i, F_DQ // 512 + 1)),
            pl.BlockSpec((t, 512), lambda i: (i, H_DV // 512)), tab, tab,
        ],
        out_specs=[
            pl.BlockSpec((t, GDN_QKV), lambda i: (i, 0)),
            pl.BlockSpec((t, SSD_XBC), lambda i: (i, 0)),
            pl.BlockSpec((t, 128), lambda i: (i, 0)),
            pl.BlockSpec((t, 128), lambda i: (i, 0)),
            pl.BlockSpec((t // GDN_CHUNK, 16, GDN_CHUNK), lambda i: (i, 0, 0)),
            pl.BlockSpec((t // SSD_CHUNK, 16, SSD_CHUNK), lambda i: (i, 0, 0)),
            pl.BlockSpec((t, 512), lambda i: (i, 0)),
            pl.BlockSpec((t, 1024), lambda i: (kv_block(i), 0)),
        ],
        out_shape=[
            jax.ShapeDtypeStruct((rows, GDN_QKV), F32),
            jax.ShapeDtypeStruct((rows, SSD_XBC), F32),
            jax.ShapeDtypeStruct((rows, 128), F32),
            jax.ShapeDtypeStruct((rows, 128), F32),
            jax.ShapeDtypeStruct((rows // GDN_CHUNK, 16, GDN_CHUNK), F32),
            jax.ShapeDtypeStruct((rows // SSD_CHUNK, 16, SSD_CHUNK), F32),
            jax.ShapeDtypeStruct((rows, 512), BF16),
            jax.ShapeDtypeStruct((rows, 1024), BF16),
        ],
        compiler_params=_cparams(("parallel",)),
        name="conv_prep_rope",
    )(f, f, f, f, cw, cb, alog, dtb, tri, f, f, hh, cos, sin)


def _when(cond, fn):
    if isinstance(cond, bool):
        if cond:
            fn()
    else:
        pl.when(cond)(fn)


def _attn_kernel(lam_ref, nw_ref, q_ref, k_ref, v_ref, *rest, tq, tk, n_q, n_k, lam_init):
    o_ref, s0, s1, t0, t1, acc_scr, m_scr = rest[-7:]
    s_scr, mt_scr = (s0, s1), (t0, t1)
    half0 = (lax.broadcasted_iota(jnp.int32, (tq, 128), 1) & (DIFF_DQK // 2)) == 0
    ones_col = (lax.broadcasted_iota(jnp.int32, (tk, 128), 1) == 0).astype(BF16)
    n_steps = n_q * n_k

    def rows_of(i, size):
        r0 = i * size
        if not isinstance(i, int):
            r0 = pl.multiple_of(r0, size)
        return pl.ds(r0, size)

    def split(n):
        return (n // n_k, n % n_k)

    def scores(n, slot):
        qi, t = split(n)
        q = q_ref[rows_of(qi, tq), :]
        zero = jnp.zeros_like(q)
        k = k_ref[rows_of(t, tk), :]
        for j, keep in enumerate((half0, jnp.logical_not(half0))):
            s = lax.dot_general(jnp.where(keep, q, zero), k, (((1,), (1,)), ((), ())),
                                preferred_element_type=F32)
            s_scr[slot][j] = s
            mt_scr[slot][j] = jnp.broadcast_to(jnp.max(s, axis=1, keepdims=True), (tq, 128))

    def consume(n, slot):
        _, t = split(n)
        v = jnp.concatenate([v_ref[rows_of(t, tk), :], ones_col], axis=1)
        m_prevs = [m_scr[0], m_scr[1]]
        accs = [acc_scr[0], acc_scr[1]]
        for j in range(2):
            m_new = jnp.maximum(m_prevs[j], mt_scr[slot][j])
            m_scr[j] = m_new
            alpha = jnp.exp2(m_prevs[j] - m_new)
            p = jnp.exp2((s_scr[slot][j]
                          - jnp.concatenate([m_new] * (tk // 128), axis=1)).astype(BF16))
            acc_scr[j] = (jnp.concatenate([alpha, alpha], axis=1) * accs[j]
                          + jnp.dot(p, v, preferred_element_type=F32))

    def finalize(qi):
        lam = lam_ref[...]
        lam_full = (jnp.exp(jnp.sum(lam[0:1] * lam[1:2], axis=-1, keepdims=True))
                    - jnp.exp(jnp.sum(lam[2:3] * lam[3:4], axis=-1, keepdims=True)) + lam_init)
        a0, a1 = acc_scr[0], acc_scr[1]
        o = (a0[:, :DIFF_DV] / a0[:, DIFF_DV:DIFF_DV + 1]
             - lam_full * (a1[:, :DIFF_DV] / a1[:, DIFF_DV:DIFF_DV + 1]))
        o_ref[rows_of(qi, tq), :] = (_rms(o) * nw_ref[...] * (1.0 - lam_init)).astype(BF16)

    def reset_max():
        m_scr[...] = jnp.full(m_scr.shape, -jnp.inf, F32)

    def region(n, slot, last=False):
        qi, t = split(n)
        _when(t == 0, reset_max)
        if not last:
            scores(n + 1, 1 - slot)
        consume(n, slot)
        _when(t == n_k - 1, lambda: finalize(qi))

    acc_scr[...] = jnp.zeros_like(acc_scr)
    scores(0, 0)
    n_pair = (n_steps - 1) // 2

    def body(u, carry):
        region(2 * u, 0)
        region(2 * u + 1, 1)
        return carry

    if n_pair > 0:
        lax.fori_loop(0, n_pair, body, 0)
    for n in range(2 * n_pair, n_steps):
        region(n, n % 2, last=n == n_steps - 1)


def _attn_scratch(tq, tk):
    return ([pltpu.VMEM((2, tq, tk), F32)] * 2 + [pltpu.VMEM((2, tq, 128), F32)] * 2
            + [pltpu.VMEM((2, tq, 2 * DIFF_DV), F32), pltpu.VMEM((2, tq, 128), F32)])


def _attention(q, kv, lam, nw, *, bsz, seq, ctx_len, lam_init, with_ctx):
    tq = 1024 if seq % 1024 == 0 else 512
    n_keys = seq + ctx_len
    tk = 768 if n_keys % 768 == 0 else ctx_len
    rows = bsz * n_keys
    out = pl.pallas_call(
        functools.partial(_attn_kernel, tq=tq, tk=tk, n_q=seq // tq, n_k=n_keys // tk,
                          lam_init=lam_init),
        grid=(bsz, DIFF_HEADS),
        in_specs=[
            _const_spec(lam.shape),
            _const_spec(nw.shape),
            pl.BlockSpec((seq, 128), lambda b, h: (b, h)),
            pl.BlockSpec((n_keys, 128), lambda b, h: (b, h)),
            pl.BlockSpec((n_keys, 128), lambda b, h: (b, DIFF_HEADS + h)),
        ],
        out_specs=pl.BlockSpec((seq, 128), lambda b, h: (b, h)),
        out_shape=jax.ShapeDtypeStruct((rows, DIFF_HEADS * DIFF_DV), BF16),
        scratch_shapes=_attn_scratch(tq, tk),
        compiler_params=_cparams(("parallel", "parallel")),
        name="diff_attn",
    )(lam, nw, q, kv, kv)
    if not with_ctx:
        return out
    qblk = bsz * seq // ctx_len
    per_sample = n_keys // ctx_len

    def ctx_keys(col):
        return lambda b, h: (b * per_sample + per_sample - 1, col + h)

    return pl.pallas_call(
        functools.partial(_attn_kernel, tq=ctx_len, tk=ctx_len, n_q=1, n_k=1, lam_init=lam_init),
        grid=(bsz, DIFF_HEADS),
        in_specs=[
            _const_spec(lam.shape),
            _const_spec(nw.shape),
            pl.BlockSpec((ctx_len, 128), lambda b, h: (qblk + b, h)),
            pl.BlockSpec((ctx_len, 128), ctx_keys(0)),
            pl.BlockSpec((ctx_len, 128), ctx_keys(DIFF_HEADS)),
            pl.BlockSpec(memory_space=pl.ANY),
        ],
        out_specs=pl.BlockSpec((ctx_len, 128), lambda b, h: (qblk + b, h)),
        out_shape=jax.ShapeDtypeStruct((rows, DIFF_HEADS * DIFF_DV), BF16),
        scratch_shapes=_attn_scratch(ctx_len, ctx_len),
        input_output_aliases={5: 0},
        compiler_params=_cparams(("parallel", "parallel")),
        name="diff_attn_ctx",
    )(lam, nw, q, kv, kv, out)


def _seq_maps(bsz, nsx, nsc):
    cbase = bsz * nsx

    def fwd(b, s):
        return jnp.where(s < nsc, cbase + b * nsc + s, b * nsx + (s - nsc))

    def bwd(b, s):
        return jnp.where(s < nsc, cbase + b * nsc + (nsc - 1 - s), b * nsx + (nsx - 1 - (s - nsc)))

    return fwd, bwd


def _gdn_kernel(cvf_ref, gbf_ref, gtf_ref, cvb_ref, gbb_ref, gtb_ref, of_ref, ob_ref, st_ref):
    @pl.when(pl.program_id(1) == 0)
    def _():
        st_ref[...] = jnp.zeros_like(st_ref)

    c_ = GDN_CHUNK
    nck = cvf_ref.shape[0] // c_
    ii = lax.broadcasted_iota(jnp.int32, (c_, c_), 0)
    jj = lax.broadcasted_iota(jnp.int32, (c_, c_), 1)
    eye = (ii == jj).astype(F32)
    blk16 = (ii // 16) == (jj // 16)
    same32 = (ii // 32) == (jj // 32)
    lvl32 = same32 & jnp.logical_not(blk16)
    lvl64 = jnp.logical_not(same32)
    dirs = (
        (cvf_ref, gbf_ref, gtf_ref, of_ref, ii >= jj, ii > jj, c_ - 1),
        (cvb_ref, gbb_ref, gtb_ref, ob_ref, ii <= jj, ii < jj, 0),
    )

    class _P:
        pass

    probs = []
    for d, (cv_ref, gb_ref, gt_ref, o_ref, incl, strict, last_row) in enumerate(dirs):
        for c in range(nck):
            cc = c if d == 0 else nck - 1 - c
            for h in range(GDN_HEADS):
                p = _P()
                p.d, p.h, p.c, p.cc = d, h, c, cc
                p.cv, p.gb, p.gt, p.o = cv_ref, gb_ref, gt_ref, o_ref
                p.incl, p.strict, p.last_row = incl, strict, last_row
                p.rows = slice(cc * c_, (cc + 1) * c_)
                p.idx = d * GDN_HEADS + h
                probs.append(p)

    def q_of(p):
        return p.cv[p.rows, p.h * GDN_DK:(p.h + 1) * GDN_DK]

    def k_of(p):
        return p.cv[p.rows, 512 + p.h * GDN_DK:512 + (p.h + 1) * GDN_DK]

    def v_of(p):
        return p.cv[p.rows, 1024 + p.h * GDN_DV:1024 + (p.h + 1) * GDN_DV]

    def gcol_of(p):
        return p.gb[p.rows, :][:, LANE_GA + p.idx:LANE_GA + p.idx + 1]

    def beta_of(p):
        return p.gb[p.rows, :][:, LANE_GB + p.idx:LANE_GB + p.idx + 1]

    def glast_of(p):
        r = p.last_row
        return p.gb[p.rows, :][r:r + 1, LANE_GA + p.idx:LANE_GA + p.idx + 1]

    for p in probs:
        kh = k_of(p)
        grow = p.gt[p.cc][p.idx:p.idx + 1, :]
        dec = jnp.where(p.incl, jnp.exp(jnp.minimum(gcol_of(p) - grow, 0.0)), 0.0)
        ga = _bdot_nt(jnp.concatenate([kh * beta_of(p), q_of(p)], axis=0), kh)
        p.m = jnp.where(p.strict, ga[:c_] * dec, 0.0)
        p.aqk = ga[c_:] * dec
    for p in probs:
        mb = jnp.where(blk16, p.m, 0.0)
        p.tinv = eye - mb
        p.sq = _bdot(mb, mb)
    for step in range(3):
        for p in probs:
            p.tinv = p.tinv + _bdot(p.tinv, p.sq)
            if step < 2:
                p.sq = _bdot(p.sq, p.sq)
    for lvl in (lvl32, lvl64):
        for p in probs:
            p.x = _bdot(jnp.where(lvl, p.m, 0.0), p.tinv)
        for p in probs:
            p.tinv = p.tinv - _bdot(p.tinv, p.x)
    for p in probs:
        beta = beta_of(p)
        rhs = jnp.concatenate([v_of(p) * beta, k_of(p) * (beta * jnp.exp(gcol_of(p)))], axis=1)
        p.uw = _bdot(p.tinv, rhs)
    states = [[st_ref[d, h] for h in range(GDN_HEADS)] for d in range(2)]
    for c in range(nck):
        cur = [p for p in probs if p.c == c]
        for p in cur:
            lhs = jnp.concatenate([p.uw[:, GDN_DV:], q_of(p) * jnp.exp(gcol_of(p))], axis=0)
            p.wq = _bdot(lhs, states[p.d][p.h])
        for p in cur:
            p.v_new = p.uw[:, :GDN_DV] - p.wq[:c_]
            glast = glast_of(p)
            k_dec = k_of(p) * jnp.exp(glast - gcol_of(p))
            states[p.d][p.h] = (states[p.d][p.h] * jnp.exp(glast)
                                + _bdot_tn(k_dec, p.v_new))
        for p in cur:
            p.out = p.wq[c_:] + _bdot(p.aqk, p.v_new)
    for p in probs:
        p.o[p.rows, p.h * GDN_DV:(p.h + 1) * GDN_DV] = p.out
    for d in range(2):
        for h in range(GDN_HEADS):
            st_ref[d, h] = states[d][h]


def _gdn(cv, gb, gt, *, bsz, nsx, nsc):
    t = SEQ_T
    rows = cv.shape[0]
    fwd, bwd = _seq_maps(bsz, nsx, nsc)
    nck = t // GDN_CHUNK

    def specs(fn):
        return [
            pl.BlockSpec((t, GDN_QKV), lambda b, s: (fn(b, s), 0)),
            pl.BlockSpec((t, 128), lambda b, s: (fn(b, s), 0)),
            pl.BlockSpec((nck, 16, GDN_CHUNK), lambda b, s: (fn(b, s), 0, 0)),
        ]

    return pl.pallas_call(
        _gdn_kernel,
        grid=(bsz, nsc + nsx),
        in_specs=specs(fwd) + specs(bwd),
        out_specs=[pl.BlockSpec((t, 512), lambda b, s: (fwd(b, s), 0)),
                   pl.BlockSpec((t, 512), lambda b, s: (bwd(b, s), 0))],
        out_shape=[jax.ShapeDtypeStruct((rows, 512), F32)] * 2,
        scratch_shapes=[pltpu.VMEM((2, GDN_HEADS, GDN_DK, GDN_DV), F32)],
        compiler_params=_cparams(("parallel", "arbitrary")),
        name="gdn_scan",
    )(cv, gb, gt, cv, gb, gt)


def _split_dot(x, sel):
    hi = x.astype(BF16)
    lo = (x - hi.astype(F32)).astype(BF16)
    return (jnp.dot(hi, sel, preferred_element_type=F32)
            + jnp.dot(lo, sel, preferred_element_type=F32))


def _ssd_kernel(dsk_ref, sel64_ref, cvf_ref, gbf_ref, dtf_ref, atf_ref, cvb_ref,
                gbb_ref, dtb_ref, atb_ref, yf_ref, yb_ref, st_ref):
    @pl.when(pl.program_id(1) == 0)
    def _():
        st_ref[...] = jnp.zeros_like(st_ref)

    c_ = SSD_CHUNK
    p_ = SSD_HEADDIM
    hpg = SSD_HEADS // SSD_GROUPS
    gw = hpg * p_
    nck = cvf_ref.shape[0] // c_
    ii = lax.broadcasted_iota(jnp.int32, (c_, c_), 0)
    jj = lax.broadcasted_iota(jnp.int32, (c_, c_), 1)
    first_half = lax.broadcasted_iota(jnp.int32, (c_, 2 * p_), 1) < p_
    dirs = (
        (cvf_ref, gbf_ref, dtf_ref, atf_ref, yf_ref, ii >= jj, c_ - 1),
        (cvb_ref, gbb_ref, dtb_ref, atb_ref, yb_ref, ii <= jj, 0),
    )

    class _P:
        pass

    probs = []
    for d, (cv_ref, gb_ref, dt_ref, at_ref, y_ref, incl, last_row) in enumerate(dirs):
        for c in range(nck):
            p = _P()
            p.d, p.c = d, c
            p.cc = c if d == 0 else nck - 1 - c
            p.rows = slice(p.cc * c_, (p.cc + 1) * c_)
            p.cv, p.y, p.incl, p.last_row = cv_ref, y_ref, incl, last_row
            p.at = at_ref[p.cc]
            p.acs, dts = gb_ref[p.rows, :], dt_ref[p.rows, :]
            p.a64 = _split_dot(p.acs, sel64_ref[d])
            xs = cv_ref[p.rows, 0:SSD_INNER]
            p.xdt = xs * _split_dot(dts, sel64_ref[d])
            alast = p.a64[last_row:last_row + 1, :]
            p.xd = p.xdt * jnp.exp(alast - p.a64)
            p.sdec = jnp.exp(alast)
            p.skip = dsk_ref[...] * xs if d == 0 else None
            probs.append(p)

    def b_of(p, g):
        return p.cv[p.rows, SSD_INNER + g * SSD_STATE:SSD_INNER + (g + 1) * SSD_STATE]

    def c_of(p, g):
        return p.cv[p.rows, SSD_INNER + (SSD_GROUPS + g) * SSD_STATE:
                    SSD_INNER + (SSD_GROUPS + g + 1) * SSD_STATE]

    states = [[st_ref[d, g] for g in range(SSD_GROUPS)] for d in range(2)]

    def chain_read(c):
        for p in probs:
            if p.c == c:
                p.y_off = [_bdot(c_of(p, g), states[p.d][g]) for g in range(SSD_GROUPS)]

    def chain_write(c):
        for p in probs:
            if p.c == c:
                for g in range(SSD_GROUPS):
                    gs = slice(g * gw, (g + 1) * gw)
                    states[p.d][g] = (states[p.d][g] * p.sdec[:, gs]
                                      + _bdot_tn(b_of(p, g), p.xd[:, gs]))

    def diag(c):
        for p in probs:
            if p.c != c:
                continue
            p.y_diag = []
            for g in range(SSD_GROUPS):
                cb = p.cb[g]
                for pair in range(hpg // 2):
                    lo_h = g * hpg + 2 * pair
                    xpair = p.xdt[:, lo_h * p_:(lo_h + 2) * p_]
                    halves = []
                    for h in (lo_h, lo_h + 1):
                        arow = p.at[p.d * SSD_HEADS + h:p.d * SSD_HEADS + h + 1, :]
                        ln = LANE_DT + p.d * SSD_HEADS + h
                        acol = p.acs[:, ln:ln + 1]
                        ldec = jnp.where(p.incl, jnp.exp(jnp.minimum(acol - arow, 0.0)), 0.0)
                        halves.append(_bdot(cb * ldec, xpair))
                    p.y_diag.append(jnp.where(first_half, halves[0], halves[1]))

    chain_read(0)
    for p in probs:
        p.cb = [_bdot_nt(c_of(p, g), b_of(p, g)) for g in range(SSD_GROUPS)]
    chain_write(0)
    diag(0)
    for c in range(1, nck):
        chain_read(c)
        diag(c)
        chain_write(c)
    for p in probs:
        y = jnp.concatenate(p.y_diag, axis=1) + jnp.concatenate(p.y_off, axis=1) * jnp.exp(p.a64)
        if p.skip is not None:
            y = y + p.skip
        p.y[p.rows, :] = y
    for d in range(2):
        for g in range(SSD_GROUPS):
            st_ref[d, g] = states[d][g]


def _ssd(dsk, sel64, cv, gb, dt, at, *, bsz, nsx, nsc):
    t = SEQ_T
    rows = cv.shape[0]
    fwd, bwd = _seq_maps(bsz, nsx, nsc)
    nck = t // SSD_CHUNK

    def specs(fn):
        return [
            pl.BlockSpec((t, SSD_XBC), lambda b, s: (fn(b, s), 0)),
            pl.BlockSpec((t, 128), lambda b, s: (fn(b, s), 0)),
            pl.BlockSpec((t, 128), lambda b, s: (fn(b, s), 0)),
            pl.BlockSpec((nck, 16, SSD_CHUNK), lambda b, s: (fn(b, s), 0, 0)),
        ]

    return pl.pallas_call(
        _ssd_kernel,
        grid=(bsz, nsc + nsx),
        in_specs=([_const_spec(dsk.shape), _const_spec(sel64.shape)]
                  + specs(fwd) + specs(bwd)),
        out_specs=[pl.BlockSpec((t, SSD_INNER), lambda b, s: (fwd(b, s), 0)),
                   pl.BlockSpec((t, SSD_INNER), lambda b, s: (bwd(b, s), 0))],
        out_shape=[jax.ShapeDtypeStruct((rows, SSD_INNER), F32)] * 2,
        scratch_shapes=[pltpu.VMEM((2, SSD_GROUPS, SSD_STATE, SSD_INNER // SSD_GROUPS), F32)],
        compiler_params=_cparams(("parallel", "arbitrary")),
        name="ssd_scan",
    )(dsk, sel64, cv, gb, dt, at, cv, gb, dt, at)


def _merge_kernel(x_ref, mod_ref, ng_ref, gof_ref, gob_ref, gz_ref, att_ref, syf_ref, syb_ref,
                  sz_ref, gate_ref, gnw_ref, snw_ref, wb_ref, wo_ref, o_ref):
    oa = gof_ref[...] + gob_ref[...]
    parts = []
    for h in range(GDN_HEADS):
        hs = slice(h * GDN_DV, (h + 1) * GDN_DV)
        parts.append(_rms(oa[:, hs]) * gnw_ref[:, hs])
    oa = jnp.concatenate(parts, axis=1) * _silu(gz_ref[...].astype(F32))
    oc = (syf_ref[...] + syb_ref[...]) * _silu(sz_ref[...].astype(F32))
    oc = _rms(oc) * snw_ref[...]
    outs = (oa.astype(BF16), att_ref[...], oc.astype(BF16))
    y = None
    for n in range(N_BRANCH):
        gate = jax.nn.sigmoid(gate_ref[:, n * D_MODEL:(n + 1) * D_MODEL].astype(F32))
        term = gate * jnp.dot(outs[n], wb_ref[n], preferred_element_type=F32)
        y = term if y is None else y + term
    y = jnp.dot(y.astype(BF16), wo_ref[...], preferred_element_type=F32)
    m = mod_ref[0]
    o_ref[...] = x_ref[...] + m[5:6] * (_rms(y) * ng_ref[3:4, :])


def _merge(s, mod, ng, gof, gob, hh, att, syf, syb, gnw, snw, wb, wo, *, tm, n_tiles,
           tiles_per_sample, n_samples):
    rows = n_tiles * tm
    row512 = pl.BlockSpec((tm, 512), lambda i: (i, 0))
    return pl.pallas_call(
        _merge_kernel,
        grid=(n_tiles,),
        in_specs=[
            pl.BlockSpec((tm, D_MODEL), lambda i: (i, 0)),
            pl.BlockSpec((1, N_MOD, D_MODEL),
                         lambda i: (jnp.minimum(i // tiles_per_sample, n_samples), 0, 0)),
            _const_spec(ng.shape),
            row512, row512,
            pl.BlockSpec((tm, 512), lambda i: (i, H_GZ // 512)),
            row512, row512, row512,
            pl.BlockSpec((tm, 512), lambda i: (i, H_SZ // 512)),
            pl.BlockSpec((tm, N_BRANCH * D_MODEL), lambda i: (i, 0)),
            _const_spec(gnw.shape),
            _const_spec(snw.shape),
            _const_spec(wb.shape),
            _const_spec(wo.shape),
        ],
        out_specs=pl.BlockSpec((tm, D_MODEL), lambda i: (i, 0)),
        out_shape=jax.ShapeDtypeStruct((rows, D_MODEL), F32),
        compiler_params=_cparams(("parallel",)),
        name="merge",
    )(s, mod, ng, gof, gob, hh, att, syf, syb, hh, hh, gnw, snw, wb, wo)


def _deinterleave_pairs(w):
    d = w.shape[0]
    w = w.reshape(d, DIFF_HEADS, 2, DIFF_DQK // 2, 2)
    return jnp.transpose(w, (0, 1, 4, 2, 3)).reshape(d, DIFF_HEADS * 2 * DIFF_DQK)


def _proj_weight(w_in):
    w_in = w_in.astype(BF16)
    o = 0
    parts = {}
    for name, width in (("gqkv", GDN_QKV), ("gz", 512), ("ga", 8), ("gb", 8), ("dq", 512),
                        ("dk", 512), ("dv", 512), ("sz", 512), ("sxbc", SSD_XBC), ("sdt", 16),
                        ("gate", N_BRANCH * D_MODEL)):
        parts[name] = w_in[:, o:o + width]
        o += width
    small_pad = jnp.zeros((w_in.shape[0], SMALL_W - 32), w_in.dtype)
    cols = [parts["gqkv"], parts["sxbc"],
            _deinterleave_pairs(parts["dq"]), _deinterleave_pairs(parts["dk"]),
            parts["ga"], parts["gb"], parts["sdt"], small_pad,
            parts["gate"], parts["dv"], parts["gz"], parts["sz"]]
    return jnp.concatenate(cols, axis=1)


def _rope_tables(seq):
    rows = seq // GRID_W
    row = jnp.repeat(jnp.arange(rows, dtype=F32), GRID_W)
    col = jnp.tile(jnp.arange(GRID_W, dtype=F32), rows)
    inv = ROPE_BASE ** (-jnp.arange(ROPE_PAIRS_PER_AXIS, dtype=F32) / ROPE_PAIRS_PER_AXIS)
    ang = jnp.concatenate([row[:, None] * inv, col[:, None] * inv], axis=-1)
    cos, sin = jnp.cos(ang), jnp.sin(ang)
    return jnp.tile(cos, (1, 4)), jnp.concatenate([-sin, -sin, sin, sin], axis=1)


def _chunk_triangles():
    i = jnp.arange(SEQ_T)[:, None]
    j = jnp.arange(SEQ_T)[None, :]
    mats = []
    for chunk in (GDN_CHUNK, SSD_CHUNK):
        same = (i // chunk) == (j // chunk)
        mats += [same & (j <= i), same & (j >= i)]
    return jnp.stack(mats).astype(BF16)


def _ssd_selector():
    rows = jnp.arange(128)[None, :, None]
    cols = jnp.arange(SSD_INNER)[None, None, :]
    d = jnp.arange(2)[:, None, None]
    return (rows == LANE_DT + d * SSD_HEADS + cols // SSD_HEADDIM).astype(BF16)


def _lane_vec(pieces):
    v = jnp.zeros((128,), F32)
    for off, val in pieces:
        v = lax.dynamic_update_slice(v, val.reshape(-1).astype(F32), (off,))
    return v.reshape(1, 128)


def kernel(x, c, ctx, c_ctx, w_ada, b_ada, norm_g, w_ffn_in, w_ffn_out, w_in, gdn_conv_w, gdn_a_log,
           gdn_dt_bias, gdn_norm_w, diff_lambda, diff_norm_w, ssd_conv_w, ssd_conv_b, ssd_a_log,
           ssd_dt_bias, ssd_d, ssd_norm_w, w_branch, w_out):
    bsz, seq, _ = x.shape
    ctx_len = ctx.shape[1]
    t = SEQ_T
    tm = 512
    assert ctx_len == t and seq % 512 == 0 and (bsz * ctx_len) % tm == 0 and seq % GRID_W == 0
    nx_rows, nc_rows = bsz * seq, bsz * ctx_len
    n_rows = nx_rows + nc_rows

    nm = -(-(bsz + 1) // 8) * 8
    cvec = jnp.zeros((nm, D_MODEL), F32).at[:bsz].set(c).at[bsz].set(c_ctx)
    mod_all = _mod_table(cvec, w_ada, b_ada)
    cos, sin = _rope_tables(seq)
    sel64 = _ssd_selector()
    tri = _chunk_triangles()

    tok = dict(tm=tm, tiles_per_sample=seq // tm, n_samples=bsz)
    all_tiles, x_tiles = n_rows // tm, nx_rows // tm
    seq_kw = dict(n_tiles=n_rows // t, nx_tiles=nx_rows // t, x_tiles_per_seq=seq // t)
    nsx, nsc = seq // t, ctx_len // t

    s = None
    for i in range(DEPTH):
        last = i == DEPTH - 1
        lam_init = 0.8 - 0.6 * math.exp(-0.3 * i)
        mod, ng = mod_all[i], norm_g[i]
        w1 = [w_ffn_in[i, j].astype(BF16) for j in range(2)]
        w2 = [w_ffn_out[i, j].astype(BF16) for j in range(2)]

        tail = ctx.reshape(nc_rows, D_MODEL) if i == 0 else None
        s = _half_ffn(x.reshape(nx_rows, D_MODEL) if i == 0 else s, mod, ng, w1[0], w2[0], slot=0,
                      gslot=0, n_tiles=all_tiles, tail=tail, **tok)

        f, hh = _in_proj(s, mod, ng, _proj_weight(w_in[i]), n_tiles=all_tiles, **tok)
        cw = jnp.concatenate([gdn_conv_w[i], ssd_conv_w[i]], axis=1)
        cb = jnp.concatenate([jnp.zeros((GDN_QKV,), F32), ssd_conv_b[i]]).reshape(1, CONV_W)
        alog = _lane_vec([(LANE_GA, gdn_a_log[i]), (LANE_DT, ssd_a_log[i])])
        dtb = _lane_vec([(LANE_GA, gdn_dt_bias[i]), (LANE_DT, ssd_dt_bias[i])])
        cvg, cvs, gb, dt, gt, at, qr, kv = _prep_rope(f, hh, cw, cb, alog, dtb, tri, cos, sin,
                                                      c_tiles_per_seq=ctx_len // t, **seq_kw)

        gof, gob = _gdn(cvg, gb, gt, bsz=bsz, nsx=nsx, nsc=nsc)
        dsk = jnp.repeat(ssd_d[i].astype(F32), SSD_HEADDIM).reshape(1, SSD_INNER)
        syf, syb = _ssd(dsk, sel64, cvs, gb, dt, at, bsz=bsz, nsx=nsx, nsc=nsc)
        att = _attention(qr, kv, diff_lambda[i], diff_norm_w[i].reshape(1, DIFF_DV), bsz=bsz,
                         seq=seq, ctx_len=ctx_len, lam_init=lam_init, with_ctx=not last)

        n_out = x_tiles if last else all_tiles
        gnw = jnp.tile(gdn_norm_w[i], GDN_HEADS).reshape(1, GDN_HEADS * GDN_DV)
        snw = ssd_norm_w[i].reshape(1, SSD_INNER)
        s = _merge(s, mod, ng, gof, gob, hh, att, syf, syb, gnw, snw, w_branch[i].astype(BF16),
                   w_out[i].astype(BF16), n_tiles=n_out, **tok)
        s = _half_ffn(s, mod, ng, w1[1], w2[1], slot=6, gslot=4, n_tiles=n_out, **tok)

    return s[:nx_rows].reshape(bsz, seq, D_MODEL)
```
